```python
import math, functools
import jax, jax.numpy as jnp
from jax import lax
import numpy as np

D_MODEL = 2048
BATCH = 2
SEQ = 4096
DEPTH = 4
DEC_BATCH = 8
DEC_SEQ = 1
PAST_LEN = 16384
PAGE_SIZE = 128

N_HEADS = 8
HEAD_DIM = 128
N_KV_HEADS = 2
W_A = N_HEADS * HEAD_DIM
IDX_HEADS = 16
IDX_DIM = 64
TOPK_MAX = 256
Q_BLOCK = 128
W_B = 1024
CONV_W = 3
W_C = 1024
CHUNK = 128
C_GROUPS = 8
C_GROUP_DIM = W_C // C_GROUPS

EPS = 1e-6
NEG = -1e30

kernel_name = "hybrid_dsa_conv_gmlp_decoder_step"


def _proj_sizes():
    kv = N_KV_HEADS * HEAD_DIM
    return [W_A, kv, kv, IDX_HEADS * IDX_DIM, IDX_DIM, IDX_HEADS, W_A,
            W_B, W_B, W_B, W_B, W_C, W_C, W_C, 3 * D_MODEL]


def split_cols(p):
    cuts = [int(c) for c in np.cumsum(_proj_sizes())[:-1]]
    return jnp.split(p, cuts, axis=-1)


def rms_norm(x, g):
    xf = x.astype(jnp.float32)
    y = xf * lax.rsqrt(jnp.mean(xf * xf, axis=-1, keepdims=True) + EPS)
    return (y * g.astype(jnp.float32)).astype(x.dtype)


def layer_norm(x, g, b):
    xf = x.astype(jnp.float32)
    mu = jnp.mean(xf, axis=-1, keepdims=True)
    var = jnp.mean(jnp.square(xf - mu), axis=-1, keepdims=True)
    y = (xf - mu) * lax.rsqrt(var + EPS)
    return (y * g.astype(jnp.float32) + b.astype(jnp.float32)).astype(x.dtype)


def index_scores(iq, ik, iw):
    s = jnp.einsum('bthd,bsd->bths', iq.astype(jnp.float32), ik.astype(jnp.float32)) * (IDX_DIM ** -0.5)
    return jnp.einsum('bths,bth->bts', jax.nn.relu(s), iw.astype(jnp.float32))


def sparse_attend(q, kg, vg, valid):
    n, t = q.shape[:2]
    qg = q.reshape(n, t, N_KV_HEADS, N_HEADS // N_KV_HEADS, HEAD_DIM)
    s = jnp.einsum('btkgd,btskd->btkgs', qg, kg).astype(jnp.float32) * (HEAD_DIM ** -0.5)
    s = jnp.where(valid[:, :, None, None, :], s, NEG)
    pr = jax.nn.softmax(s, axis=-1).astype(vg.dtype)
    o = jnp.einsum('btkgs,btskd->btkgd', pr, vg)
    return o.reshape(n, t, N_HEADS * HEAD_DIM)


def dsa_prompt(q, k, v, iq, ik, iw):
    n, s_len = q.shape[:2]
    topk = min(TOPK_MAX, s_len // 4)
    nb = s_len // Q_BLOCK
    gather = jax.vmap(lambda rows, ii: rows[ii])

    def to_blocks(a):
        return a.reshape((a.shape[0], nb, Q_BLOCK) + a.shape[2:]).swapaxes(0, 1)

    key_pos = jnp.arange(s_len)

    def blk(args):
        qb, iqb, iwb, t0 = args
        pos = t0 + jnp.arange(Q_BLOCK)
        sc = index_scores(iqb, ik, iwb)
        sc = jnp.where((key_pos[None, :] <= pos[:, None])[None], sc, NEG)
        _, idx = lax.top_k(sc, topk)
        kg = gather(k, idx)
        vg = gather(v, idx)
        valid = idx <= pos[None, :, None]
        return sparse_attend(qb, kg, vg, valid)

    out = lax.map(blk, (to_blocks(q), to_blocks(iq), to_blocks(iw),
                        jnp.arange(nb, dtype=jnp.int32) * Q_BLOCK))
    return out.swapaxes(0, 1).reshape(n, s_len, W_A)


def dsa_sample(q, k, v, iq, ik, iw, pool_k, pool_v, pool_ik, page_table):
    n, t = q.shape[:2]
    n_pages = page_table.shape[1]
    past = n_pages * PAGE_SIZE
    l_vis = past + t
    topk = min(TOPK_MAX, l_vis // 4)
    ik_past = pool_ik[page_table].reshape(n, past, IDX_DIM)
    ik_all = jnp.concatenate([ik_past, ik.astype(ik_past.dtype)], axis=1)
    sc = index_scores(iq, ik_all, iw)
    pos = past + jnp.arange(t)
    sc = jnp.where((jnp.arange(l_vis)[None, :] <= pos[:, None])[None], sc, NEG)
    _, idx = lax.top_k(sc, topk)
    in_past = idx < past
    pi = jnp.minimum(idx, past - 1)
    phys = jnp.take_along_axis(page_table, (pi // PAGE_SIZE).reshape(n, -1), axis=1).reshape(pi.shape)
    off = pi % PAGE_SIZE
    ni = jnp.clip(idx - past, 0, t - 1)
    gather = jax.vmap(lambda rows, ii: rows[ii])
    kg = jnp.where(in_past[..., None, None], pool_k[phys, off], gather(k, ni).astype(pool_k.dtype))
    vg = jnp.where(in_past[..., None, None], pool_v[phys, off], gather(v, ni).astype(pool_v.dtype))
    valid = idx <= pos[None, :, None]
    return sparse_attend(q, kg.astype(q.dtype), vg.astype(q.dtype), valid)


def short_conv(z, buf, w):
    t = z.shape[1]
    zz = jnp.concatenate([buf.astype(z.dtype), z], axis=1)
    y = w[0] * zz[:, 0:t]
    for j in range(1, CONV_W):
        y = y + w[j] * zz[:, j:j + t]
    return y, zz[:, -(CONV_W - 1):]


def chunk_mix(v, ln_g, ln_b, ws, bs):
    n, t, _ = v.shape
    vn = layer_norm(v, ln_g, ln_b)
    nc = -(-t // CHUNK)
    vp = jnp.pad(vn, ((0, 0), (0, nc * CHUNK - t), (0, 0))).reshape(n, nc, CHUNK, C_GROUPS, C_GROUP_DIM)
    mixed = jnp.einsum('gts,ncsgd->nctgd', jnp.tril(ws), vp) + bs.T[None, None, :, :, None]
    return mixed.reshape(n, nc * CHUNK, W_C)[:, :t], vn


def hybrid_layer(x, lp, attn_fn, conv_buf):
    g_pre, g_post, w_in, conv_w, ln_g, ln_b, ws, bs, w_br, w_o = lp
    n, t, _ = x.shape
    xn = rms_norm(x, g_pre)
    p = jnp.einsum('btd,de->bte', xn, w_in)
    q, k, v, iq, ik, iw, ga, h, bg, cg, gb, u, vc, gc, r = split_cols(p)
    q = q.reshape(n, t, N_HEADS, HEAD_DIM)
    k = k.reshape(n, t, N_KV_HEADS, HEAD_DIM)
    v = v.reshape(n, t, N_KV_HEADS, HEAD_DIM)
    iq = iq.reshape(n, t, IDX_HEADS, IDX_DIM)
    iw = iw * (IDX_HEADS ** -0.5)
    ya = attn_fn(q, k, v, iq, ik, iw) * jax.nn.silu(ga)
    conv_out, new_buf = short_conv(cg * h, conv_buf, conv_w)
    yb = bg * conv_out * jax.nn.silu(gb)
    mixed, vn = chunk_mix(vc, ln_g, ln_b, ws, bs)
    yc = u * mixed * jax.nn.silu(gc)
    ra, rb, rc = jnp.split(jax.nn.sigmoid(r), 3, axis=-1)
    m = (ra * (ya @ w_br[:W_A]) + rb * (yb @ w_br[W_A:W_A + W_B]) + rc * (yc @ w_br[W_A + W_B:]))
    out = m @ w_o
    return x + rms_norm(out, g_post), (k, v, ik), new_buf, vn


def setup_inputs(seed: int = 0) -> dict:
    key = jax.random.key(seed)
    ks = jax.random.split(key, 20)
    n_pages = PAST_LEN // PAGE_SIZE
    used = DEC_BATCH * n_pages
    n_pool = used + max(1, used // 4)
    n_in = sum(_proj_sizes())
    f32 = jnp.float32

    def nrm(k, shape, scale):
        return jax.random.normal(k, shape, f32) * scale

    perm = jax.random.permutation(ks[0], n_pool)
    page_table = perm[:used].reshape(DEC_BATCH, n_pages).astype(jnp.int32)
    return {
        "x_prompt": nrm(ks[1], (BATCH, SEQ, D_MODEL), 1.0),
        "x_sample": nrm(ks[2], (DEC_BATCH, DEC_SEQ, D_MODEL), 1.0),
        "cache_k": nrm(ks[3], (DEPTH, n_pool, PAGE_SIZE, N_KV_HEADS, HEAD_DIM), 1.0),
        "cache_v": nrm(ks[4], (DEPTH, n_pool, PAGE_SIZE, N_KV_HEADS, HEAD_DIM), 1.0),
        "cache_idx_k": nrm(ks[5], (DEPTH, n_pool, PAGE_SIZE, IDX_DIM), 1.0),
        "state_conv": nrm(ks[6], (DEPTH, DEC_BATCH, CONV_W - 1, W_B), 1.0),
        "page_table": page_table,
        "g_pre": 1.0 + nrm(ks[7], (DEPTH, D_MODEL), 0.01),
        "g_post": 1.0 + nrm(ks[8], (DEPTH, D_MODEL), 0.01),
        "w_in": nrm(ks[9], (DEPTH, D_MODEL, n_in), D_MODEL ** -0.5),
        "conv_w": nrm(ks[10], (DEPTH, CONV_W, W_B), CONV_W ** -0.5),
        "sgu_ln_g": 1.0 + nrm(ks[11], (DEPTH, W_C), 0.01),
        "sgu_ln_b": nrm(ks[12], (DEPTH, W_C), 0.01),
        "sgu_w": nrm(ks[13], (DEPTH, C_GROUPS, CHUNK, CHUNK), CHUNK ** -0.5),
        "sgu_b": 1.0 + nrm(ks[14], (DEPTH, C_GROUPS, CHUNK), 0.01),
        "w_branch": nrm(ks[15], (DEPTH, W_A + W_B + W_C, D_MODEL), W_A ** -0.5),
        "w_out": nrm(ks[16], (DEPTH, D_MODEL, D_MODEL), D_MODEL ** -0.5),
    }


def reference(x_prompt, x_sample, cache_k, cache_v, cache_idx_k, state_conv, page_table,
              g_pre, g_post, w_in, conv_w, sgu_ln_g, sgu_ln_b, sgu_w, sgu_b, w_branch, w_out):
    xp, xs = x_prompt, x_sample
    nb, sp = xp.shape[:2]
    kp_l, vp_l, ikp_l, cp_l = [], [], [], []
    ks_l, vs_l, iks_l, cs_l, chs_l = [], [], [], [], []
    for l in range(DEPTH):
        lp = (g_pre[l], g_post[l], w_in[l], conv_w[l], sgu_ln_g[l], sgu_ln_b[l],
              sgu_w[l], sgu_b[l], w_branch[l], w_out[l])
        xp, (kp, vp, ikp), cbp, _ = hybrid_layer(
            xp, lp, dsa_prompt, jnp.zeros((nb, CONV_W - 1, W_B), xp.dtype))
        kp_l.append(kp.reshape(nb, sp // PAGE_SIZE, PAGE_SIZE, N_KV_HEADS, HEAD_DIM))
        vp_l.append(vp.reshape(nb, sp // PAGE_SIZE, PAGE_SIZE, N_KV_HEADS, HEAD_DIM))
        ikp_l.append(ikp.reshape(nb, sp // PAGE_SIZE, PAGE_SIZE, IDX_DIM))
        cp_l.append(cbp)
        attn_s = functools.partial(dsa_sample, pool_k=cache_k[l], pool_v=cache_v[l],
                                   pool_ik=cache_idx_k[l], page_table=page_table)
        xs, (kn, vn, ikn), cbs, vcs = hybrid_layer(xs, lp, attn_s, state_conv[l])
        ks_l.append(kn)
        vs_l.append(vn)
        iks_l.append(ikn)
        cs_l.append(cbs)
        chs_l.append(vcs)
    return (xp, xs,
            jnp.stack(kp_l), jnp.stack(vp_l), jnp.stack(ikp_l),
            jnp.stack(ks_l), jnp.stack(vs_l), jnp.stack(iks_l),
            jnp.stack(cp_l), jnp.stack(cs_l), jnp.stack(chs_l))
```

```python
import functools

import jax
import jax.numpy as jnp
from jax import lax
from jax.experimental import pallas as pl
from jax.experimental.pallas import tpu as pltpu

N_HEADS = 8
HEAD_DIM = 128
N_KV_HEADS = 2
GROUP = N_HEADS // N_KV_HEADS
W_A = N_HEADS * HEAD_DIM
KV = N_KV_HEADS * HEAD_DIM
IDX_HEADS = 16
IDX_DIM = 64
TOPK_MAX = 256
Q_BLOCK = 128
PAGE_SIZE = 128
W_B = 1024
CONV_W = 3
W_C = 1024
CHUNK = 128
C_GROUPS = 8
EPS = 1e-6
NEG = -1e30
BIG = 3e38

LANE = 128
KEY_CHUNK = 256
PAGES_PER_STEP = 8
INT_MIN = -(2 ** 31)

KVI_W = KV + KV + 3 * LANE
KVI_IK = 2 * KV
KVI_IW = 2 * KV + 2 * LANE

F32 = jnp.float32
BF16 = jnp.bfloat16


def _silu(x):
    return x / (1.0 + jnp.exp(-x))


def _sigmoid(x):
    return 1.0 / (1.0 + jnp.exp(-x))


def _dot_nt(a, b):
    return lax.dot_general(a, b, (((1,), (1,)), ((), ())), preferred_element_type=F32)


def _ordered_key(x):
    bits = pltpu.bitcast(x, jnp.int32)
    return jnp.where(bits < 0, bits ^ jnp.int32(0x7FFFFFFF), bits)


def _key_to_float(key):
    bits = jnp.where(key < 0, key ^ jnp.int32(0x7FFFFFFF), key)
    return pltpu.bitcast(bits, F32)


def _kth_largest_key(count_ge, shape, k):
    zero = jnp.zeros(shape, jnp.int32)
    t = jnp.where(count_ge(zero) >= k, zero, jnp.full(shape, INT_MIN, jnp.int32))

    def body(it, t):
        cand = t | jnp.left_shift(jnp.int32(1), jnp.int32(30) - it)
        return jnp.where(count_ge(cand) >= k, cand, t)

    return lax.fori_loop(0, 31, body, t)


def _norm_matmul_kernel(x_ref, g_ref, w_ref, *rest, n_out):
    out_refs = rest[:n_out]
    xn_ref = rest[n_out]

    @pl.when(pl.program_id(1) == 0)
    def _():
        x = x_ref[...]
        ms = jnp.mean(x * x, axis=-1, keepdims=True)
        xn_ref[...] = (x * lax.rsqrt(ms + EPS) * g_ref[...]).astype(xn_ref.dtype)

    acc = jnp.dot(xn_ref[...].astype(BF16), w_ref[...], preferred_element_type=F32)
    for o_ref in out_refs:
        o_ref[...] = acc.astype(o_ref.dtype)


def _norm_matmul(x, g, w, out_dtypes, tm, tn):
    m, d = x.shape
    n = w.shape[1]
    tm = min(tm, m)
    tn = min(tn, n)
    while n % tn:
        tn //= 2
    assert m % tm == 0 and tn % LANE == 0
    xn_dtype = BF16 if tm % 16 == 0 else F32
    outs = pl.pallas_call(
        functools.partial(_norm_matmul_kernel, n_out=len(out_dtypes)),
        grid=(m // tm, n // tn),
        in_specs=[
            pl.BlockSpec((tm, d), lambda i, j: (i, 0)),
            pl.BlockSpec((1, d), lambda i, j: (0, 0)),
            pl.BlockSpec((d, tn), lambda i, j: (0, j)),
        ],
        out_specs=[pl.BlockSpec((tm, tn), lambda i, j: (i, j)) for _ in out_dtypes],
        out_shape=[jax.ShapeDtypeStruct((m, n), dt) for dt in out_dtypes],
        scratch_shapes=[pltpu.VMEM((tm, d), xn_dtype)],
        compiler_params=pltpu.CompilerParams(
            dimension_semantics=("parallel", "arbitrary"),
            vmem_limit_bytes=56 * 1024 * 1024),
        name="norm_matmul",
    )(x, g.reshape(1, d), w)
    return outs


def _dsa_prompt_kernel(q_ref, iq_ref, w_ref, k_ref, v_ref, ik_ref, ga_ref, o_ref,
                       s_ref, key_ref, iqs_ref, qg_ref, wb_ref, m_ref, l_ref, acc_ref,
                       *, topk):
    i = pl.program_id(1)
    qb = q_ref.shape[0]
    kc = KEY_CHUNK
    nk = (i + 1) * qb
    nch = (nk + kc - 1) // kc
    pos = i * qb + lax.broadcasted_iota(jnp.int32, (qb, kc), 0)
    lane = lax.broadcasted_iota(jnp.int32, (qb, kc), 1)

    n_tiles = IDX_HEADS // 2
    for j in range(n_tiles):
        iqs_ref[j * qb:(j + 1) * qb, :] = iq_ref[:, j * LANE:(j + 1) * LANE]
    wscale = (IDX_HEADS ** -0.5) * (IDX_DIM ** -0.5)
    for h in range(IDX_HEADS):
        wb_ref[h] = jnp.broadcast_to(w_ref[:, h:h + 1] * wscale, (qb, kc))
    for g in range(N_KV_HEADS):
        for hl in range(GROUP):
            h = g * GROUP + hl
            qg_ref[g, hl * qb:(hl + 1) * qb, :] = q_ref[:, h * HEAD_DIM:(h + 1) * HEAD_DIM]

    def score_body(c, carry):
        off = pl.multiple_of(c * kc, kc)
        acc = jnp.zeros((qb, kc), F32)
        for half in range(2):
            res = _dot_nt(iqs_ref[...], ik_ref[pl.ds(off, kc), half * LANE:(half + 1) * LANE])
            for j in range(n_tiles):
                acc = acc + jnp.maximum(res[j * qb:(j + 1) * qb], 0.0) * wb_ref[2 * j + half]
        sc = jnp.where(off + lane <= pos, acc, NEG)
        s_ref[:, pl.ds(off, kc)] = sc
        key_ref[:, pl.ds(off, kc)] = _ordered_key(sc)
        return carry

    lax.fori_loop(0, nch, score_body, 0)

    def count_ge(cand):
        def body(c, acc):
            off = pl.multiple_of(c * kc, kc)
            return acc + jnp.where(key_ref[:, pl.ds(off, kc)] >= cand, 1.0, 0.0)
        acc = lax.fori_loop(0, nch, body, jnp.zeros((qb, kc), F32))
        return jnp.sum(acc, axis=-1, keepdims=True)

    thr = _key_to_float(_kth_largest_key(count_ge, (qb, 1), float(topk)))

    def bias_body(c, carry):
        off = pl.multiple_of(c * kc, kc)
        sel = (s_ref[:, pl.ds(off, kc)] >= thr) & (off + lane <= pos)
        s_ref[:, pl.ds(off, kc)] = jnp.where(sel, BIG, NEG)
        return carry

    lax.fori_loop(0, nch, bias_body, 0)

    scale = HEAD_DIM ** -0.5
    for g in range(N_KV_HEADS):
        m_ref[...] = jnp.full(m_ref.shape, -jnp.inf, F32)
        l_ref[...] = jnp.zeros(l_ref.shape, F32)
        acc_ref[...] = jnp.zeros(acc_ref.shape, F32)

        def attn_body(c, carry, g=g):
            off = pl.multiple_of(c * kc, kc)
            kt = k_ref[pl.ds(off, kc), g * HEAD_DIM:(g + 1) * HEAD_DIM]
            vt = v_ref[pl.ds(off, kc), g * HEAD_DIM:(g + 1) * HEAD_DIM]
            s = _dot_nt(qg_ref[g], kt) * scale
            bound = s_ref[:, pl.ds(off, kc)]
            s = jnp.minimum(s, jnp.concatenate([bound] * GROUP, axis=0))
            m_prev = m_ref[...]
            m_new = jnp.maximum(m_prev, jnp.max(s, axis=-1, keepdims=True))
            alpha = jnp.exp(m_prev - m_new)
            p = jnp.exp(s - m_new)
            l_ref[...] = alpha * l_ref[...] + jnp.sum(p, axis=-1, keepdims=True)
            acc_ref[...] = alpha * acc_ref[...] + jnp.dot(
                p.astype(BF16), vt, preferred_element_type=F32)
            m_ref[...] = m_new
            return carry

        lax.fori_loop(0, nch, attn_body, 0)
        o = acc_ref[...] / l_ref[...]
        for hl in range(GROUP):
            h = g * GROUP + hl
            cols = slice(h * HEAD_DIM, (h + 1) * HEAD_DIM)
            gate = _silu(ga_ref[:, cols].astype(F32))
            o_ref[:, cols] = (o[hl * qb:(hl + 1) * qb] * gate).astype(o_ref.dtype)


def _dsa_prompt(qiq, kvi_b, kvi_f, rest, batch, seq):
    m = qiq.shape[0]
    qb = Q_BLOCK
    nq = seq // qb
    topk = min(TOPK_MAX, seq // 4)
    assert seq % KEY_CHUNK == 0 and topk <= KEY_CHUNK
    row = lambda b, i: b * nq + i
    return pl.pallas_call(
        functools.partial(_dsa_prompt_kernel, topk=topk),
        grid=(batch, nq),
        in_specs=[
            pl.BlockSpec((qb, W_A), lambda b, i: (row(b, i), 0)),
            pl.BlockSpec((qb, IDX_HEADS * IDX_DIM), lambda b, i: (row(b, i), 1)),
            pl.BlockSpec((qb, LANE), lambda b, i: (row(b, i), KVI_IW // LANE)),
            pl.BlockSpec((seq, KV), lambda b, i: (b, 0)),
            pl.BlockSpec((seq, KV), lambda b, i: (b, 1)),
            pl.BlockSpec((seq, 2 * LANE), lambda b, i: (b, KVI_IK // (2 * LANE))),
            pl.BlockSpec((qb, W_A), lambda b, i: (row(b, i), 0)),
        ],
        out_specs=pl.BlockSpec((qb, W_A), lambda b, i: (row(b, i), 0)),
        out_shape=jax.ShapeDtypeStruct((m, W_A), BF16),
        scratch_shapes=[
            pltpu.VMEM((qb, seq), F32),
            pltpu.VMEM((qb, seq), jnp.int32),
            pltpu.VMEM((IDX_HEADS // 2 * qb, LANE), BF16),
            pltpu.VMEM((N_KV_HEADS, GROUP * qb, HEAD_DIM), BF16),
            pltpu.VMEM((IDX_HEADS, qb, KEY_CHUNK), F32),
            pltpu.VMEM((GROUP * qb, 1), F32),
            pltpu.VMEM((GROUP * qb, 1), F32),
            pltpu.VMEM((GROUP * qb, HEAD_DIM), F32),
        ],
        compiler_params=pltpu.CompilerParams(
            dimension_semantics=("parallel", "arbitrary"),
            vmem_limit_bytes=56 * 1024 * 1024),
        name="dsa_prompt",
    )(qiq, qiq, kvi_f, kvi_b, kvi_b, kvi_b, rest)


PREV_ROWS = 16


def _bc_prompt_kernel(h_ref, bg_ref, cg_ref, gb_ref, u_ref, vc_ref, gc_ref, ph_ref, pcg_ref,
                      cw_ref, lg_ref, lb_ref, ws_ref, bst_ref,
                      yb_ref, yc_ref, zt_ref, zs_ref):
    tm = h_ref.shape[0]
    j = pl.program_id(1)
    z = cg_ref[...].astype(F32) * h_ref[...].astype(F32)
    pz = pcg_ref[...].astype(F32) * ph_ref[...].astype(F32)
    pz = jnp.where(j == 0, 0.0, pz)
    zs_ref[0:PREV_ROWS, :] = pz
    zs_ref[PREV_ROWS:PREV_ROWS + tm, :] = z
    z1 = zs_ref[PREV_ROWS - 1:PREV_ROWS - 1 + tm, :]
    z2 = zs_ref[PREV_ROWS - 2:PREV_ROWS - 2 + tm, :]
    conv = cw_ref[0:1, :] * z2 + cw_ref[1:2, :] * z1 + cw_ref[2:3, :] * z
    yb = bg_ref[...].astype(F32) * conv * _silu(gb_ref[...].astype(F32))
    yb_ref[...] = yb.astype(yb_ref.dtype)
    zt_ref[...] = z[tm - 8:tm]

    vc = vc_ref[...].astype(F32)
    mu = jnp.mean(vc, axis=-1, keepdims=True)
    dv = vc - mu
    var = jnp.mean(dv * dv, axis=-1, keepdims=True)
    vn = (dv * lax.rsqrt(var + EPS) * lg_ref[...] + lb_ref[...]).astype(BF16)
    rr = lax.broadcasted_iota(jnp.int32, (CHUNK, CHUNK), 0)
    cc = lax.broadcasted_iota(jnp.int32, (CHUNK, CHUNK), 1)
    gdim = W_C // C_GROUPS
    for g in range(C_GROUPS):
        wg = jnp.where(rr >= cc, ws_ref[g], 0.0).astype(BF16)
        bias = bst_ref[:, g:g + 1]
        cols = slice(g * gdim, (g + 1) * gdim)
        for c in range(tm // CHUNK):
            rows = slice(c * CHUNK, (c + 1) * CHUNK)
            mixed = jnp.dot(wg, vn[rows, cols], preferred_element_type=F32) + bias
            yc = u_ref[rows, cols].astype(F32) * mixed * _silu(gc_ref[rows, cols].astype(F32))
            yc_ref[rows, cols] = yc.astype(yc_ref.dtype)


def _bc_prompt(rest, conv_w, ln_g, ln_b, ws, bs, batch, seq, tm):
    m = rest.shape[0]
    nt = seq // tm
    row = lambda b, j: b * nt + j
    prev = lambda b, j: jnp.maximum((b * seq + j * tm) // PREV_ROWS - 1, 0)
    col = lambda c: (lambda b, j: (row(b, j), c))
    pcol = lambda c: (lambda b, j: (prev(b, j), c))
    const2 = lambda b, j: (0, 0)
    yb, yc, zt = pl.pallas_call(
        _bc_prompt_kernel,
        grid=(batch, nt),
        in_specs=[pl.BlockSpec((tm, W_B), col(c)) for c in range(1, 8)] + [
            pl.BlockSpec((PREV_ROWS, W_B), pcol(1)),
            pl.BlockSpec((PREV_ROWS, W_B), pcol(3)),
            pl.BlockSpec((CONV_W, W_B), const2),
            pl.BlockSpec((1, W_C), const2),
            pl.BlockSpec((1, W_C), const2),
            pl.BlockSpec((C_GROUPS, CHUNK, CHUNK), lambda b, j: (0, 0, 0)),
            pl.BlockSpec((CHUNK, C_GROUPS), const2),
        ],
        out_specs=[
            pl.BlockSpec((tm, W_B), lambda b, j: (row(b, j), 0)),
            pl.BlockSpec((tm, W_C), lambda b, j: (row(b, j), 0)),
            pl.BlockSpec((8, W_B), lambda b, j: (b, 0)),
        ],
        out_shape=[
            jax.ShapeDtypeStruct((m, W_B), BF16),
            jax.ShapeDtypeStruct((m, W_C), BF16),
            jax.ShapeDtypeStruct((batch * 8, W_B), F32),
        ],
        scratch_shapes=[pltpu.VMEM((tm + PREV_ROWS, W_B), F32)],
        compiler_params=pltpu.CompilerParams(
            dimension_semantics=("parallel", "arbitrary"),
            vmem_limit_bytes=56 * 1024 * 1024),
        name="bc_prompt",
    )(*([rest] * 7), rest, rest, conv_w, ln_g.reshape(1, W_C), ln_b.reshape(1, W_C), ws, bs.T)
    return yb, yc, zt


def _merge_kernel(ya_ref, yb_ref, yc_ref, wa_ref, wb_ref, wc_ref, ra_ref, rb_ref, rc_ref, o_ref):
    def branch(y_ref, w_ref, r_ref):
        proj = jnp.dot(y_ref[...].astype(BF16), w_ref[...], preferred_element_type=F32)
        return _sigmoid(r_ref[...].astype(F32)) * proj

    o_ref[...] = (branch(ya_ref, wa_ref, ra_ref) + branch(yb_ref, wb_ref, rb_ref)
                  + branch(yc_ref, wc_ref, rc_ref)).astype(o_ref.dtype)


def _merge(ya, yb, yc, w_br, rest, d, tm, tn):
    m = ya.shape[0]
    tm = min(tm, m)
    nb = d // tn
    r0 = 8 * W_A // tn
    y_spec = pl.BlockSpec((tm, W_A), lambda i, j: (i, 0))
    w_spec = lambda b: pl.BlockSpec((W_A, tn), lambda i, j: (b, j))
    r_spec = lambda b: pl.BlockSpec((tm, tn), lambda i, j: (i, r0 + b * nb + j))
    return pl.pallas_call(
        _merge_kernel,
        grid=(m // tm, nb),
        in_specs=[y_spec, y_spec, y_spec, w_spec(0), w_spec(1), w_spec(2),
                  r_spec(0), r_spec(1), r_spec(2)],
        out_specs=pl.BlockSpec((tm, tn), lambda i, j: (i, j)),
        out_shape=jax.ShapeDtypeStruct((m, d), rest.dtype),
        compiler_params=pltpu.CompilerParams(
            dimension_semantics=("parallel", "arbitrary"),
            vmem_limit_bytes=56 * 1024 * 1024),
        name="merge",
    )(ya, yb, yc, w_br, w_br, w_br, rest, rest, rest)


def _outproj_kernel(m_ref, w_ref, x_ref, g_ref, o_ref):
    out = jnp.dot(m_ref[...].astype(BF16), w_ref[...], preferred_element_type=F32)
    ms = jnp.mean(out * out, axis=-1, keepdims=True)
    o_ref[...] = x_ref[...] + out * lax.rsqrt(ms + EPS) * g_ref[...]


def _outproj(mm, w_o, x, g, tm):
    m, d = x.shape
    tm = min(tm, m)
    return pl.pallas_call(
        _outproj_kernel,
        grid=(m // tm,),
        in_specs=[
            pl.BlockSpec((tm, d), lambda i: (i, 0)),
            pl.BlockSpec((d, d), lambda i: (0, 0)),
            pl.BlockSpec((tm, d), lambda i: (i, 0)),
            pl.BlockSpec((1, d), lambda i: (0, 0)),
        ],
        out_specs=pl.BlockSpec((tm, d), lambda i: (i, 0)),
        out_shape=jax.ShapeDtypeStruct((m, d), F32),
        compiler_params=pltpu.CompilerParams(
            dimension_semantics=("parallel",),
            vmem_limit_bytes=56 * 1024 * 1024),
        name="outproj",
    )(mm, w_o, x, g.reshape(1, d))


def _sample_scores_kernel(pt_ref, *refs):
    del pt_ref
    npg = PAGES_PER_STEP
    page_refs = refs[:npg]
    iq_ref, w_ref, o_ref = refs[npg:]
    iq = iq_ref[...].astype(BF16)
    w = w_ref[...] * ((IDX_HEADS ** -0.5) * (IDX_DIM ** -0.5))
    for u in range(npg):
        s = _dot_nt(iq, page_refs[u][...].astype(BF16))
        o_ref[u:u + 1, :] = jnp.sum(jnp.maximum(s, 0.0) * w, axis=0, keepdims=True)


def _sample_scores(layer, page_table, cache_ik, iq3, w3):
    nb, n_pages = page_table.shape
    npg = PAGES_PER_STEP
    assert n_pages % npg == 0
    page_spec = lambda u: pl.BlockSpec(
        (None, None, PAGE_SIZE, IDX_DIM),
        lambda s, j, pt: (layer, pt[s, j * npg + u], 0, 0))
    grid_spec = pltpu.PrefetchScalarGridSpec(
        num_scalar_prefetch=1,
        grid=(nb, n_pages // npg),
        in_specs=[page_spec(u) for u in range(npg)] + [
            pl.BlockSpec((None, IDX_HEADS, IDX_DIM), lambda s, j, pt: (s, 0, 0)),
            pl.BlockSpec((None, IDX_HEADS, 1), lambda s, j, pt: (s, 0, 0)),
        ],
        out_specs=pl.BlockSpec((None, npg, PAGE_SIZE), lambda s, j, pt: (s, j, 0)),
    )
    return pl.pallas_call(
        _sample_scores_kernel,
        grid_spec=grid_spec,
        out_shape=jax.ShapeDtypeStruct((nb, n_pages, PAGE_SIZE), F32),
        compiler_params=pltpu.CompilerParams(dimension_semantics=("parallel", "arbitrary")),
        name="sample_scores",
    )(page_table, *([cache_ik] * npg), iq3, w3)


def _sample_select_kernel(sc_ref, iq_ref, ikn_ref, w_ref, sel_ref, selself_ref, *, topk):
    nb = sc_ref.shape[0]
    w = w_ref[...] * ((IDX_HEADS ** -0.5) * (IDX_DIM ** -0.5))
    s_self = jnp.sum(iq_ref[...] * ikn_ref[...], axis=-1, keepdims=True)
    sc_self = jnp.sum(jnp.maximum(s_self, 0.0) * w, axis=1, keepdims=True)
    sc = sc_ref[...]
    key = _ordered_key(sc)
    key_self = _ordered_key(sc_self)

    def count_ge(cand):
        ge = jnp.where(key >= cand, 1.0, 0.0)
        cnt = jnp.sum(jnp.sum(ge, axis=2, keepdims=True), axis=1, keepdims=True)
        return cnt + jnp.where(key_self >= cand, 1.0, 0.0)

    thr = _key_to_float(_kth_largest_key(count_ge, (nb, 1, 1), float(topk)))
    sel_ref[...] = jnp.where(sc >= thr, 1.0, 0.0)
    selself_ref[...] = jnp.broadcast_to(jnp.where(sc_self >= thr, 1.0, 0.0), selself_ref.shape)


def _sample_select(scores, iq3, ikn3, w3, topk):
    nb, n_pages, _ = scores.shape
    return pl.pallas_call(
        functools.partial(_sample_select_kernel, topk=topk),
        out_shape=[jax.ShapeDtypeStruct((nb, n_pages, PAGE_SIZE), F32),
                   jax.ShapeDtypeStruct((nb, 1, LANE), F32)],
        name="sample_select",
    )(scores, iq3, ikn3, w3)


def _sample_attn_kernel(pt_ref, *refs):
    del pt_ref
    npg = PAGES_PER_STEP
    k_refs = refs[:npg]
    v_refs = refs[npg:2 * npg]
    sel_ref, selself_ref, q_ref, kn_ref, vn_ref, o_ref, m_ref, l_ref, acc_ref = refs[2 * npg:]
    j = pl.program_id(1)
    scale = HEAD_DIM ** -0.5
    rows2 = N_KV_HEADS * PAGE_SIZE

    @pl.when(j == 0)
    def _():
        m_ref[...] = jnp.full(m_ref.shape, -jnp.inf, F32)
        l_ref[...] = jnp.zeros(l_ref.shape, F32)
        acc_ref[...] = jnp.zeros(acc_ref.shape, F32)

    q = q_ref[...].astype(BF16)
    er = lax.broadcasted_iota(jnp.int32, (PAGE_SIZE, rows2), 0)
    ec = lax.broadcasted_iota(jnp.int32, (PAGE_SIZE, rows2), 1)
    expand = jnp.where(ec // N_KV_HEADS == er, 1.0, 0.0).astype(BF16)
    sel2 = jnp.dot(sel_ref[...].astype(BF16), expand, preferred_element_type=F32)
    hh = lax.broadcasted_iota(jnp.int32, (N_HEADS, rows2), 0)
    cc = lax.broadcasted_iota(jnp.int32, (N_HEADS, rows2), 1)
    head_ok = (hh // GROUP) == (cc % N_KV_HEADS)

    def update(s, v):
        m_prev = m_ref[...]
        m_new = jnp.maximum(m_prev, jnp.max(s, axis=-1, keepdims=True))
        alpha = jnp.exp(m_prev - m_new)
        p = jnp.exp(s - m_new)
        l_ref[...] = alpha * l_ref[...] + jnp.sum(p, axis=-1, keepdims=True)
        acc_ref[...] = alpha * acc_ref[...] + v(p)
        m_ref[...] = m_new

    for u in range(npg):
        kp = k_refs[u][...].astype(BF16)
        vp = v_refs[u][...].astype(BF16)
        s = _dot_nt(q, kp) * scale
        s = jnp.where(head_ok & (sel2[u:u + 1, :] > 0.5), s, NEG)
        update(s, lambda p, vp=vp: jnp.dot(p.astype(BF16), vp, preferred_element_type=F32))

    @pl.when(j == pl.num_programs(1) - 1)
    def _():
        hrow = lax.broadcasted_iota(jnp.int32, (N_HEADS, HEAD_DIM), 0)
        kn = kn_ref[...]
        vn = vn_ref[...]
        k_self = jnp.where(hrow < GROUP, kn[0:1, :], kn[1:2, :])
        v_self = jnp.where(hrow < GROUP, vn[0:1, :], vn[1:2, :])
        s_self = jnp.sum(q_ref[...] * k_self, axis=-1, keepdims=True) * scale
        s_self = jnp.where(selself_ref[0:1, 0:1] > 0.5, s_self, NEG)
        update(s_self, lambda p: p * v_self)
        o_ref[...] = acc_ref[...] / l_ref[...]


def _sample_attn(layer, page_table, cache_k4, cache_v4, sel, selself, q3, kn3, vn3):
    nb, n_pages = page_table.shape
    npg = PAGES_PER_STEP
    rows2 = N_KV_HEADS * PAGE_SIZE
    page_spec = lambda u: pl.BlockSpec(
        (None, None, rows2, HEAD_DIM),
        lambda s, j, pt: (layer, pt[s, j * npg + u], 0, 0))
    per_seq = lambda shape: pl.BlockSpec((None,) + shape, lambda s, j, pt: (s, 0, 0))
    grid_spec = pltpu.PrefetchScalarGridSpec(
        num_scalar_prefetch=1,
        grid=(nb, n_pages // npg),
        in_specs=[page_spec(u) for u in range(npg)] * 2 + [
            pl.BlockSpec((None, npg, PAGE_SIZE), lambda s, j, pt: (s, j, 0)),
            per_seq((1, LANE)),
            per_seq((N_HEADS, HEAD_DIM)),
            per_seq((N_KV_HEADS, HEAD_DIM)),
            per_seq((N_KV_HEADS, HEAD_DIM)),
        ],
        out_specs=per_seq((N_HEADS, HEAD_DIM)),
        scratch_shapes=[
            pltpu.VMEM((N_HEADS, 1), F32),
            pltpu.VMEM((N_HEADS, 1), F32),
            pltpu.VMEM((N_HEADS, HEAD_DIM), F32),
        ],
    )
    return pl.pallas_call(
        _sample_attn_kernel,
        grid_spec=grid_spec,
        out_shape=jax.ShapeDtypeStruct((nb, N_HEADS, HEAD_DIM), F32),
        compiler_params=pltpu.CompilerParams(dimension_semantics=("parallel", "arbitrary")),
        name="sample_attn",
    )(page_table, *([cache_k4] * npg), *([cache_v4] * npg), sel, selself, q3, kn3, vn3)


def _sample_mid_kernel(att_ref, rest_ref, s0_ref, s1_ref, cw_ref, lg_ref, lb_ref, w0_ref, b0_ref,
                       ya_ref, yb_ref, yc_ref, z_ref, vn_ref):
    blk = lambda c: rest_ref[:, c * W_A:(c + 1) * W_A]
    ga, h, bg, cg, gb, u, vc, gc = [blk(c) for c in range(8)]
    ya_ref[...] = att_ref[...] * _silu(ga)
    z = cg * h
    conv = cw_ref[0:1, :] * s0_ref[...] + cw_ref[1:2, :] * s1_ref[...] + cw_ref[2:3, :] * z
    yb_ref[...] = bg * conv * _silu(gb)
    z_ref[...] = z
    mu = jnp.mean(vc, axis=-1, keepdims=True)
    dv = vc - mu
    var = jnp.mean(dv * dv, axis=-1, keepdims=True)
    vn = dv * lax.rsqrt(var + EPS) * lg_ref[...] + lb_ref[...]
    vn_ref[...] = vn
    mixed = w0_ref[...] * vn + b0_ref[...]
    yc_ref[...] = u * mixed * _silu(gc)


def _sample_mid(att, rest, s0, s1, conv_w, ln_g, ln_b, w0, b0):
    nb = att.shape[0]
    shp = jax.ShapeDtypeStruct((nb, W_A), F32)
    return pl.pallas_call(
        _sample_mid_kernel,
        out_shape=[shp] * 5,
        name="sample_mid",
    )(att, rest, s0, s1, conv_w, ln_g.reshape(1, W_C), ln_b.reshape(1, W_C), w0, b0)


def _pack_w_in(w_in):
    d = w_in.shape[0]
    cuts = {}
    off = 0
    for name, size in (("q", W_A), ("k", KV), ("v", KV), ("iq", IDX_HEADS * IDX_DIM),
                       ("ik", IDX_DIM), ("iw", IDX_HEADS)):
        cuts[name] = w_in[:, off:off + size]
        off += size
    z = lambda n: jnp.zeros((d, n), w_in.dtype)
    w_qiq = jnp.concatenate([cuts["q"], cuts["iq"]], axis=1)
    w_kvi = jnp.concatenate([cuts["k"], cuts["v"],
                             cuts["ik"], z(LANE - IDX_DIM),
                             z(LANE - IDX_DIM), cuts["ik"],
                             cuts["iw"], z(LANE - IDX_HEADS)], axis=1)
    w_rest = w_in[:, off:]
    return w_qiq.astype(BF16), w_kvi.astype(BF16), w_rest.astype(BF16)


def kernel(x_prompt, x_sample, cache_k, cache_v, cache_idx_k, state_conv, page_table,
           g_pre, g_post, w_in, conv_w, sgu_ln_g, sgu_ln_b, sgu_w, sgu_b, w_branch, w_out):
    depth = w_in.shape[0]
    batch, seq, d = x_prompt.shape
    nb = x_sample.shape[0]
    n_pool = cache_k.shape[1]
    n_pages = page_table.shape[1]
    topk_s = min(TOPK_MAX, (n_pages * PAGE_SIZE + 1) // 4)
    assert x_sample.shape[1] == 1

    xp = x_prompt.reshape(batch * seq, d)
    xs = x_sample.reshape(nb, d)
    cache_k4 = cache_k.reshape(depth, n_pool, PAGE_SIZE * N_KV_HEADS, HEAD_DIM)
    cache_v4 = cache_v.reshape(depth, n_pool, PAGE_SIZE * N_KV_HEADS, HEAD_DIM)

    outs = {name: [] for name in ("kp", "vp", "ikp", "ks", "vs", "iks", "cp", "cs", "chs")}
    for l in range(depth):
        w_qiq, w_kvi, w_rest = _pack_w_in(w_in[l])
        w_br = w_branch[l].astype(BF16)
        w_o = w_out[l].astype(BF16)

        (qiq,) = _norm_matmul(xp, g_pre[l], w_qiq, (BF16,), 1024, 1024)
        kvi_f, kvi_b = _norm_matmul(xp, g_pre[l], w_kvi, (F32, BF16), 1024, KVI_W)
        (rest,) = _norm_matmul(xp, g_pre[l], w_rest, (BF16,), 1024, 1024)
        ya = _dsa_prompt(qiq, kvi_b, kvi_f, rest, batch, seq)
        yb, yc, zt = _bc_prompt(rest, conv_w[l], sgu_ln_g[l], sgu_ln_b[l], sgu_w[l], sgu_b[l],
                                batch, seq, min(512, seq))
        mm = _merge(ya, yb, yc, w_br, rest, d, 1024, min(512, d))
        xp = _outproj(mm, w_o, xp, g_post[l], 512)
        outs["kp"].append(kvi_f[:, 0:KV].reshape(batch, seq // PAGE_SIZE, PAGE_SIZE,
                                                 N_KV_HEADS, HEAD_DIM))
        outs["vp"].append(kvi_f[:, KV:2 * KV].reshape(batch, seq // PAGE_SIZE, PAGE_SIZE,
                                                      N_KV_HEADS, HEAD_DIM))
        outs["ikp"].append(kvi_f[:, KVI_IK:KVI_IK + IDX_DIM].reshape(
            batch, seq // PAGE_SIZE, PAGE_SIZE, IDX_DIM))
        outs["cp"].append(zt.reshape(batch, 8, W_B)[:, 8 - (CONV_W - 1):])

        (qiq_s,) = _norm_matmul(xs, g_pre[l], w_qiq, (F32,), 8, 1024)
        (kvi_s,) = _norm_matmul(xs, g_pre[l], w_kvi, (F32,), 8, KVI_W)
        (rest_s,) = _norm_matmul(xs, g_pre[l], w_rest, (F32,), 8, 1024)
        q3 = qiq_s[:, :W_A].reshape(nb, N_HEADS, HEAD_DIM)
        iq3 = qiq_s[:, W_A:].reshape(nb, IDX_HEADS, IDX_DIM)
        kn = kvi_s[:, 0:KV]
        vn = kvi_s[:, KV:2 * KV]
        ikn = kvi_s[:, KVI_IK:KVI_IK + IDX_DIM]
        w3 = kvi_s[:, KVI_IW:KVI_IW + IDX_HEADS].reshape(nb, IDX_HEADS, 1)
        scores = _sample_scores(l, page_table, cache_idx_k, iq3, w3)
        sel, selself = _sample_select(scores, iq3, ikn.reshape(nb, 1, IDX_DIM), w3, topk_s)
        att = _sample_attn(l, page_table, cache_k4, cache_v4, sel, selself, q3,
                           kn.reshape(nb, N_KV_HEADS, HEAD_DIM),
                           vn.reshape(nb, N_KV_HEADS, HEAD_DIM))
        gdim = W_C // C_GROUPS
        w0 = jnp.repeat(sgu_w[l][:, 0, 0], gdim).reshape(1, W_C)
        b0 = jnp.repeat(sgu_b[l][:, 0], gdim).reshape(1, W_C)
        ya_s, yb_s, yc_s, z_s, vn_s = _sample_mid(
            att.reshape(nb, W_A), rest_s, state_conv[l][:, 0], state_conv[l][:, 1],
            conv_w[l], sgu_ln_g[l], sgu_ln_b[l], w0, b0)
        mm_s = _merge(ya_s, yb_s, yc_s, w_br, rest_s, d, 8, min(512, d))
        xs = _outproj(mm_s, w_o, xs, g_post[l], 8)
        outs["ks"].append(kn.reshape(nb, 1, N_KV_HEADS, HEAD_DIM))
        outs["vs"].append(vn.reshape(nb, 1, N_KV_HEADS, HEAD_DIM))
        outs["iks"].append(ikn.reshape(nb, 1, IDX_DIM))
        outs["cs"].append(jnp.stack([state_conv[l][:, 1], z_s], axis=1))
        outs["chs"].append(vn_s.reshape(nb, 1, W_C))

    st = lambda name: jnp.stack(outs[name])
    return (xp.reshape(batch, seq, d), xs.reshape(nb, 1, d),
            st("kp"), st("vp"), st("ikp"), st("ks"), st("vs"), st("iks"),
            st("cp"), st("cs"), st("chs"))
```

```python
import functools

import jax
import jax.numpy as jnp
from jax import lax
from jax.experimental import pallas as pl
from jax.experimental.pallas import tpu as pltpu

N_HEADS = 8
HEAD_DIM = 128
N_KV_HEADS = 2
GROUP = N_HEADS // N_KV_HEADS
W_A = N_HEADS * HEAD_DIM
KV = N_KV_HEADS * HEAD_DIM
IDX_HEADS = 16
IDX_DIM = 64
TOPK_MAX = 256
Q_BLOCK = 128
PAGE_SIZE = 128
W_B = 1024
CONV_W = 3
W_C = 1024
CHUNK = 128
C_GROUPS = 8
EPS = 1e-6
NEG = -1e30
BIG = 3e38

LANE = 128
KEY_CHUNK = 256
PAGES_PER_STEP = 16
INT_MIN = -(2 ** 31)
LOG2_E = 1.4426950408889634
BF16_SUBLANES = 16
VT_ROWS = HEAD_DIM + BF16_SUBLANES
COUNT_ROWS = 64

KVI_W = KV + KV + 3 * LANE
KVI_IK = 2 * KV
KVI_IW = 2 * KV + 2 * LANE

F32 = jnp.float32
BF16 = jnp.bfloat16


def _silu(x):
    return x / (1.0 + jnp.exp(-x))


def _sigmoid(x):
    return 1.0 / (1.0 + jnp.exp(-x))


def _dot_nt(a, b):
    return lax.dot_general(a, b, (((1,), (1,)), ((), ())), preferred_element_type=F32)


def _ordered_key(x):
    bits = pltpu.bitcast(x, jnp.int32)
    return jnp.where(bits < 0, bits ^ jnp.int32(0x7FFFFFFF), bits)


def _key_to_float(key):
    bits = jnp.where(key < 0, key ^ jnp.int32(0x7FFFFFFF), key)
    return pltpu.bitcast(bits, F32)


def _kth_largest_key(count_ge, shape, k):
    zero = jnp.zeros(shape, jnp.int32)
    t = jnp.where(count_ge(zero) >= k, zero, jnp.full(shape, INT_MIN, jnp.int32))

    def body(it, t):
        cand = t | jnp.left_shift(jnp.int32(1), jnp.int32(30) - it)
        return jnp.where(count_ge(cand) >= k, cand, t)

    return lax.fori_loop(0, 31, body, t)


def _norm_matmul_kernel(x_ref, g_ref, w_ref, *rest, n_out):
    out_refs = rest[:n_out]
    xn_ref = rest[n_out]

    @pl.when(pl.program_id(1) == 0)
    def _():
        x = x_ref[...]
        ms = jnp.mean(x * x, axis=-1, keepdims=True)
        xn_ref[...] = (x * lax.rsqrt(ms + EPS) * g_ref[...]).astype(xn_ref.dtype)

    acc = jnp.dot(xn_ref[...].astype(BF16), w_ref[...], preferred_element_type=F32)
    for o_ref in out_refs:
        o_ref[...] = acc.astype(o_ref.dtype)


def _norm_matmul(x, g, w, out_dtypes, tm, tn):
    m, d = x.shape
    n = w.shape[1]
    tm = min(tm, m)
    tn = min(tn, n)
    while n % tn:
        tn //= 2
    assert m % tm == 0 and tn % LANE == 0
    xn_dtype = BF16 if tm % 16 == 0 else F32
    outs = pl.pallas_call(
        functools.partial(_norm_matmul_kernel, n_out=len(out_dtypes)),
        grid=(m // tm, n // tn),
        in_specs=[
            pl.BlockSpec((tm, d), lambda i, j: (i, 0)),
            pl.BlockSpec((1, d), lambda i, j: (0, 0)),
            pl.BlockSpec((d, tn), lambda i, j: (0, j)),
        ],
        out_specs=[pl.BlockSpec((tm, tn), lambda i, j: (i, j)) for _ in out_dtypes],
        out_shape=[jax.ShapeDtypeStruct((m, n), dt) for dt in out_dtypes],
        scratch_shapes=[pltpu.VMEM((tm, d), xn_dtype)],
        compiler_params=pltpu.CompilerParams(
            dimension_semantics=("parallel", "arbitrary"),
            vmem_limit_bytes=56 * 1024 * 1024),
        name="norm_matmul",
    )(x, g.reshape(1, d), w)
    return outs


def _transpose_bf16(x):
    return x.astype(F32).T.astype(BF16)


def _dsa_prompt_kernel(q_ref, iq_ref, w_ref, k_ref, v_ref, ik_ref, ga_ref, o_ref,
                       st_ref, key_ref, iqt_ref, qgt_ref, wt_ref, vt_ref,
                       m0_ref, m1_ref, acc0_ref, acc1_ref, sba_ref, sbb_ref, *, topk):
    i = pl.program_id(1)
    qb = q_ref.shape[0]
    seq = k_ref.shape[0]
    kc = KEY_CHUNK
    nk = (i + 1) * qb
    nch = (nk + kc - 1) // kc
    qpos = i * qb + lax.broadcasted_iota(jnp.int32, (kc, qb), 1)
    krow = lax.broadcasted_iota(jnp.int32, (kc, qb), 0)

    @pl.when(i == 0)
    def _():
        def body(c, carry):
            off = pl.multiple_of(c * kc, kc)
            vt = _transpose_bf16(v_ref[pl.ds(off, kc), :])
            for g in range(N_KV_HEADS):
                vt_ref[g, 0:HEAD_DIM, pl.ds(off, kc)] = vt[g * HEAD_DIM:(g + 1) * HEAD_DIM]
                vt_ref[g, HEAD_DIM:VT_ROWS, pl.ds(off, kc)] = jnp.ones(
                    (VT_ROWS - HEAD_DIM, kc), BF16)
            return carry
        lax.fori_loop(0, seq // kc, body, 0)

    n_tiles = IDX_HEADS // 2
    for j in range(n_tiles):
        iqt_ref[:, j * qb:(j + 1) * qb] = _transpose_bf16(iq_ref[:, j * LANE:(j + 1) * LANE])
    for g in range(N_KV_HEADS):
        for hl in range(GROUP):
            h = g * GROUP + hl
            qgt_ref[g, :, hl * qb:(hl + 1) * qb] = _transpose_bf16(
                q_ref[:, h * HEAD_DIM:(h + 1) * HEAD_DIM])
    wt_ref[...] = (w_ref[...] * ((IDX_HEADS ** -0.5) * (IDX_DIM ** -0.5))).T

    def score_body(c, carry):
        off = pl.multiple_of(c * kc, kc)
        acc = jnp.zeros((kc, qb), F32)
        for half in range(2):
            res = jnp.dot(ik_ref[pl.ds(off, kc), half * LANE:(half + 1) * LANE], iqt_ref[...],
                          preferred_element_type=F32)
            for j in range(n_tiles):
                h = 2 * j + half
                acc = acc + jnp.maximum(res[:, j * qb:(j + 1) * qb], 0.0) * wt_ref[h:h + 1, :]
        sc = jnp.where(off + krow <= qpos, acc, NEG)
        st_ref[pl.ds(off, kc), :] = sc
        key_ref[pl.ds(off, kc), :] = _ordered_key(sc)
        return carry

    lax.fori_loop(0, nch, score_body, 0)

    def count_ge(cand):
        def body(c, acc):
            off = pl.multiple_of(c * kc, kc)
            ge = jnp.where(key_ref[pl.ds(off, kc), :] >= cand, 1.0, 0.0)
            return acc + jnp.sum(ge.reshape(kc // COUNT_ROWS, COUNT_ROWS, qb), axis=0)
        acc = lax.fori_loop(0, nch, body, jnp.zeros((COUNT_ROWS, qb), F32))
        return jnp.sum(acc, axis=0, keepdims=True)

    thr = _key_to_float(_kth_largest_key(count_ge, (1, qb), float(topk)))

    def bias_body(c, carry):
        off = pl.multiple_of(c * kc, kc)
        sel = (st_ref[pl.ds(off, kc), :] >= thr) & (off + krow <= qpos)
        st_ref[pl.ds(off, kc), :] = jnp.where(sel, BIG, NEG)
        return carry

    lax.fori_loop(0, nch, bias_body, 0)

    exp2_scale = (HEAD_DIM ** -0.5) * LOG2_E
    m_refs = (m0_ref, m1_ref)
    acc_refs = (acc0_ref, acc1_ref)
    for g in range(N_KV_HEADS):
        m_refs[g][...] = jnp.full(m_refs[g].shape, -jnp.inf, F32)
        acc_refs[g][...] = jnp.zeros(acc_refs[g].shape, F32)

    def logits(c, dst_ref):
        off = pl.multiple_of(jnp.minimum(c, nch - 1) * kc, kc)
        for g in range(N_KV_HEADS):
            dst_ref[g] = jnp.dot(
                k_ref[pl.ds(off, kc), g * HEAD_DIM:(g + 1) * HEAD_DIM], qgt_ref[g],
                preferred_element_type=F32)

    def softmax_pv(c, src_ref):
        off = pl.multiple_of(jnp.minimum(c, nch - 1) * kc, kc)
        bound = jnp.where(c < nch, st_ref[pl.ds(off, kc), :], NEG)
        bound = jnp.concatenate([bound] * GROUP, axis=1)
        for g in range(N_KV_HEADS):
            s = jnp.minimum(src_ref[g], bound)
            m_prev = m_refs[g][...]
            m_new = jnp.maximum(m_prev, jnp.max(s, axis=0, keepdims=True))
            alpha = jnp.exp2((m_prev - m_new) * exp2_scale)
            p = jnp.exp2((s - m_new) * exp2_scale)
            acc_refs[g][...] = alpha * acc_refs[g][...] + jnp.dot(
                vt_ref[g, :, pl.ds(off, kc)], p.astype(BF16), preferred_element_type=F32)
            m_refs[g][...] = m_new

    logits(0, sba_ref)

    def attn_body(c2, carry):
        c = 2 * c2
        logits(c + 1, sbb_ref)
        softmax_pv(c, sba_ref)
        logits(c + 2, sba_ref)
        softmax_pv(c + 1, sbb_ref)
        return carry

    lax.fori_loop(0, (nch + 1) // 2, attn_body, 0)
    for g in range(N_KV_HEADS):
        acc = acc_refs[g][...]
        o_t = acc[0:HEAD_DIM] / acc[HEAD_DIM:HEAD_DIM + 1]
        for hl in range(GROUP):
            h = g * GROUP + hl
            cols = slice(h * HEAD_DIM, (h + 1) * HEAD_DIM)
            gate = _silu(ga_ref[:, cols].astype(F32))
            o_ref[:, cols] = (o_t[:, hl * qb:(hl + 1) * qb].T * gate).astype(o_ref.dtype)


def _dsa_prompt(qiq, kvi_b, kvi_f, rest, batch, seq):
    m = qiq.shape[0]
    qb = Q_BLOCK
    nq = seq // qb
    topk = min(TOPK_MAX, seq // 4)
    assert seq % KEY_CHUNK == 0 and topk <= KEY_CHUNK
    row = lambda b, i: b * nq + i
    return pl.pallas_call(
        functools.partial(_dsa_prompt_kernel, topk=topk),
        grid=(batch, nq),
        in_specs=[
            pl.BlockSpec((qb, W_A), lambda b, i: (row(b, i), 0)),
            pl.BlockSpec((qb, IDX_HEADS * IDX_DIM), lambda b, i: (row(b, i), 1)),
            pl.BlockSpec((qb, LANE), lambda b, i: (row(b, i), KVI_IW // LANE)),
            pl.BlockSpec((seq, KV), lambda b, i: (b, 0)),
            pl.BlockSpec((seq, KV), lambda b, i: (b, 1)),
            pl.BlockSpec((seq, 2 * LANE), lambda b, i: (b, KVI_IK // (2 * LANE))),
            pl.BlockSpec((qb, W_A), lambda b, i: (row(b, i), 0)),
        ],
        out_specs=pl.BlockSpec((qb, W_A), lambda b, i: (row(b, i), 0)),
        out_shape=jax.ShapeDtypeStruct((m, W_A), BF16),
        scratch_shapes=[
            pltpu.VMEM((seq, qb), F32),
            pltpu.VMEM((seq, qb), jnp.int32),
            pltpu.VMEM((LANE, IDX_HEADS // 2 * qb), BF16),
            pltpu.VMEM((N_KV_HEADS, HEAD_DIM, GROUP * qb), BF16),
            pltpu.VMEM((LANE, qb), F32),
            pltpu.VMEM((N_KV_HEADS, VT_ROWS, seq), BF16),
            pltpu.VMEM((1, GROUP * qb), F32),
            pltpu.VMEM((1, GROUP * qb), F32),
            pltpu.VMEM((VT_ROWS, GROUP * qb), F32),
            pltpu.VMEM((VT_ROWS, GROUP * qb), F32),
            pltpu.VMEM((N_KV_HEADS, KEY_CHUNK, GROUP * qb), F32),
            pltpu.VMEM((N_KV_HEADS, KEY_CHUNK, GROUP * qb), F32),
        ],
        compiler_params=pltpu.CompilerParams(
            dimension_semantics=("parallel", "arbitrary"),
            vmem_limit_bytes=56 * 1024 * 1024),
        name="dsa_prompt",
    )(qiq, qiq, kvi_f, kvi_b, kvi_b, kvi_b, rest)


PREV_ROWS = 16


def _bc_prompt_kernel(h_ref, bg_ref, cg_ref, gb_ref, u_ref, vc_ref, gc_ref, ph_ref, pcg_ref,
                      cw_ref, lg_ref, lb_ref, ws_ref, bst_ref,
                      yb_ref, yc_ref, zt_ref, zs_ref):
    tm = h_ref.shape[0]
    j = pl.program_id(1)
    z = cg_ref[...].astype(F32) * h_ref[...].astype(F32)
    pz = pcg_ref[...].astype(F32) * ph_ref[...].astype(F32)
    pz = jnp.where(j == 0, 0.0, pz)
    zs_ref[0:PREV_ROWS, :] = pz
    zs_ref[PREV_ROWS:PREV_ROWS + tm, :] = z
    z1 = zs_ref[PREV_ROWS - 1:PREV_ROWS - 1 + tm, :]
    z2 = zs_ref[PREV_ROWS - 2:PREV_ROWS - 2 + tm, :]
    conv = cw_ref[0:1, :] * z2 + cw_ref[1:2, :] * z1 + cw_ref[2:3, :] * z
    yb = bg_ref[...].astype(F32) * conv * _silu(gb_ref[...].astype(F32))
    yb_ref[...] = yb.astype(yb_ref.dtype)
    zt_ref[...] = z[tm - 8:tm]

    vc = vc_ref[...].astype(F32)
    mu = jnp.mean(vc, axis=-1, keepdims=True)
    dv = vc - mu
    var = jnp.mean(dv * dv, axis=-1, keepdims=True)
    vn = (dv * lax.rsqrt(var + EPS) * lg_ref[...] + lb_ref[...]).astype(BF16)
    rr = lax.broadcasted_iota(jnp.int32, (CHUNK, CHUNK), 0)
    cc = lax.broadcasted_iota(jnp.int32, (CHUNK, CHUNK), 1)
    gdim = W_C // C_GROUPS
    for g in range(C_GROUPS):
        wg = jnp.where(rr >= cc, ws_ref[g], 0.0).astype(BF16)
        bias = bst_ref[:, g:g + 1]
        cols = slice(g * gdim, (g + 1) * gdim)
        for c in range(tm // CHUNK):
            rows = slice(c * CHUNK, (c + 1) * CHUNK)
            mixed = jnp.dot(wg, vn[rows, cols], preferred_element_type=F32) + bias
            yc = u_ref[rows, cols].astype(F32) * mixed * _silu(gc_ref[rows, cols].astype(F32))
            yc_ref[rows, cols] = yc.astype(yc_ref.dtype)


def _bc_prompt(rest, conv_w, ln_g, ln_b, ws, bs, batch, seq, tm):
    m = rest.shape[0]
    nt = seq // tm
    row = lambda b, j: b * nt + j
    prev = lambda b, j: jnp.maximum((b * seq + j * tm) // PREV_ROWS - 1, 0)
    col = lambda c: (lambda b, j: (row(b, j), c))
    pcol = lambda c: (lambda b, j: (prev(b, j), c))
    const2 = lambda b, j: (0, 0)
    yb, yc, zt = pl.pallas_call(
        _bc_prompt_kernel,
        grid=(batch, nt),
        in_specs=[pl.BlockSpec((tm, W_B), col(c)) for c in range(1, 8)] + [
            pl.BlockSpec((PREV_ROWS, W_B), pcol(1)),
            pl.BlockSpec((PREV_ROWS, W_B), pcol(3)),
            pl.BlockSpec((CONV_W, W_B), const2),
            pl.BlockSpec((1, W_C), const2),
            pl.BlockSpec((1, W_C), const2),
            pl.BlockSpec((C_GROUPS, CHUNK, CHUNK), lambda b, j: (0, 0, 0)),
            pl.BlockSpec((CHUNK, C_GROUPS), const2),
        ],
        out_specs=[
            pl.BlockSpec((tm, W_B), lambda b, j: (row(b, j), 0)),
            pl.BlockSpec((tm, W_C), lambda b, j: (row(b, j), 0)),
            pl.BlockSpec((8, W_B), lambda b, j: (b, 0)),
        ],
        out_shape=[
            jax.ShapeDtypeStruct((m, W_B), BF16),
            jax.ShapeDtypeStruct((m, W_C), BF16),
            jax.ShapeDtypeStruct((batch * 8, W_B), F32),
        ],
        scratch_shapes=[pltpu.VMEM((tm + PREV_ROWS, W_B), F32)],
        compiler_params=pltpu.CompilerParams(
            dimension_semantics=("parallel", "arbitrary"),
            vmem_limit_bytes=56 * 1024 * 1024),
        name="bc_prompt",
    )(*([rest] * 7), rest, rest, conv_w, ln_g.reshape(1, W_C), ln_b.reshape(1, W_C), ws, bs.T)
    return yb, yc, zt


def _merge_kernel(ya_ref, yb_ref, yc_ref, wa_ref, wb_ref, wc_ref, ra_ref, rb_ref, rc_ref, o_ref):
    def branch(y_ref, w_ref, r_ref):
        proj = jnp.dot(y_ref[...].astype(BF16), w_ref[...], preferred_element_type=F32)
        return _sigmoid(r_ref[...].astype(F32)) * proj

    o_ref[...] = (branch(ya_ref, wa_ref, ra_ref) + branch(yb_ref, wb_ref, rb_ref)
                  + branch(yc_ref, wc_ref, rc_ref)).astype(o_ref.dtype)


def _merge(ya, yb, yc, w_br, rest, d, tm, tn):
    m = ya.shape[0]
    tm = min(tm, m)
    nb = d // tn
    r0 = 8 * W_A // tn
    y_spec = pl.BlockSpec((tm, W_A), lambda i, j: (i, 0))
    w_spec = lambda b: pl.BlockSpec((W_A, tn), lambda i, j: (b, j))
    r_spec = lambda b: pl.BlockSpec((tm, tn), lambda i, j: (i, r0 + b * nb + j))
    return pl.pallas_call(
        _merge_kernel,
        grid=(m // tm, nb),
        in_specs=[y_spec, y_spec, y_spec, w_spec(0), w_spec(1), w_spec(2),
                  r_spec(0), r_spec(1), r_spec(2)],
        out_specs=pl.BlockSpec((tm, tn), lambda i, j: (i, j)),
        out_shape=jax.ShapeDtypeStruct((m, d), rest.dtype),
        compiler_params=pltpu.CompilerParams(
            dimension_semantics=("parallel", "arbitrary"),
            vmem_limit_bytes=56 * 1024 * 1024),
        name="merge",
    )(ya, yb, yc, w_br, w_br, w_br, rest, rest, rest)


def _outproj_kernel(m_ref, w_ref, x_ref, g_ref, o_ref):
    out = jnp.dot(m_ref[...].astype(BF16), w_ref[...], preferred_element_type=F32)
    ms = jnp.mean(out * out, axis=-1, keepdims=True)
    o_ref[...] = x_ref[...] + out * lax.rsqrt(ms + EPS) * g_ref[...]


def _outproj(mm, w_o, x, g, tm):
    m, d = x.shape
    tm = min(tm, m)
    return pl.pallas_call(
        _outproj_kernel,
        grid=(m // tm,),
        in_specs=[
            pl.BlockSpec((tm, d), lambda i: (i, 0)),
            pl.BlockSpec((d, d), lambda i: (0, 0)),
            pl.BlockSpec((tm, d), lambda i: (i, 0)),
            pl.BlockSpec((1, d), lambda i: (0, 0)),
        ],
        out_specs=pl.BlockSpec((tm, d), lambda i: (i, 0)),
        out_shape=jax.ShapeDtypeStruct((m, d), F32),
        compiler_params=pltpu.CompilerParams(
            dimension_semantics=("parallel",),
            vmem_limit_bytes=56 * 1024 * 1024),
        name="outproj",
    )(mm, w_o, x, g.reshape(1, d))


def _sample_scores_kernel(pt_ref, *refs):
    del pt_ref
    npg = PAGES_PER_STEP
    page_refs = refs[:npg]
    iq_ref, w_ref, o_ref = refs[npg:]
    iq = iq_ref[...].astype(BF16)
    w = w_ref[...] * ((IDX_HEADS ** -0.5) * (IDX_DIM ** -0.5))
    for u in range(npg):
        s = jnp.dot(iq, page_refs[u][...].astype(BF16),
                    preferred_element_type=F32)
        o_ref[u:u + 1, :] = jnp.sum(jnp.maximum(s, 0.0) * w, axis=0, keepdims=True)


def _sample_scores(layer, page_table, cache_ik_t, iq3, w3):
    nb, n_pages = page_table.shape
    npg = PAGES_PER_STEP
    assert n_pages % npg == 0
    page_spec = lambda u: pl.BlockSpec(
        (None, None, IDX_DIM, PAGE_SIZE),
        lambda s, j, pt: (layer, pt[s, j * npg + u], 0, 0))
    grid_spec = pltpu.PrefetchScalarGridSpec(
        num_scalar_prefetch=1,
        grid=(nb, n_pages // npg),
        in_specs=[page_spec(u) for u in range(npg)] + [
            pl.BlockSpec((None, IDX_HEADS, IDX_DIM), lambda s, j, pt: (s, 0, 0)),
            pl.BlockSpec((None, IDX_HEADS, 1), lambda s, j, pt: (s, 0, 0)),
        ],
        out_specs=pl.BlockSpec((None, npg, PAGE_SIZE), lambda s, j, pt: (s, j, 0)),
    )
    return pl.pallas_call(
        _sample_scores_kernel,
        grid_spec=grid_spec,
        out_shape=jax.ShapeDtypeStruct((nb, n_pages, PAGE_SIZE), F32),
        compiler_params=pltpu.CompilerParams(dimension_semantics=("parallel", "arbitrary")),
        name="sample_scores",
    )(page_table, *([cache_ik_t] * npg), iq3, w3)


def _sample_select_kernel(sc_ref, iq_ref, ikn_ref, w_ref, sel_ref, selself_ref, *, topk):
    nb = sc_ref.shape[0]
    w = w_ref[...] * ((IDX_HEADS ** -0.5) * (IDX_DIM ** -0.5))
    s_self = jnp.sum(iq_ref[...] * ikn_ref[...], axis=-1, keepdims=True)
    sc_self = jnp.sum(jnp.maximum(s_self, 0.0) * w, axis=1, keepdims=True)
    sc = sc_ref[...]
    key = _ordered_key(sc)
    key_self = _ordered_key(sc_self)

    def count_ge(cand):
        ge = jnp.where(key >= cand, 1.0, 0.0)
        cnt = jnp.sum(jnp.sum(ge, axis=2, keepdims=True), axis=1, keepdims=True)
        return cnt + jnp.where(key_self >= cand, 1.0, 0.0)

    thr = _key_to_float(_kth_largest_key(count_ge, (nb, 1, 1), float(topk)))
    sel_ref[...] = jnp.where(sc >= thr, 1.0, 0.0)
    selself_ref[...] = jnp.broadcast_to(jnp.where(sc_self >= thr, 1.0, 0.0), selself_ref.shape)


def _sample_select(scores, iq3, ikn3, w3, topk):
    nb, n_pages, _ = scores.shape
    return pl.pallas_call(
        functools.partial(_sample_select_kernel, topk=topk),
        out_shape=[jax.ShapeDtypeStruct((nb, n_pages, PAGE_SIZE), F32),
                   jax.ShapeDtypeStruct((nb, 1, LANE), F32)],
        name="sample_select",
    )(scores, iq3, ikn3, w3)


def _sample_attn_kernel(pt_ref, *refs):
    del pt_ref
    npg = PAGES_PER_STEP
    k_refs = refs[:npg]
    v_refs = refs[npg:2 * npg]
    sel_ref, selself_ref, q_ref, kn_ref, vn_ref, o_ref, m_ref, l_ref, acc_ref = refs[2 * npg:]
    j = pl.program_id(1)
    scale = HEAD_DIM ** -0.5
    rows2 = N_KV_HEADS * PAGE_SIZE

    @pl.when(j == 0)
    def _():
        m_ref[...] = jnp.full(m_ref.shape, -jnp.inf, F32)
        l_ref[...] = jnp.zeros(l_ref.shape, F32)
        acc_ref[...] = jnp.zeros(acc_ref.shape, F32)

    q = q_ref[...].astype(BF16)
    er = lax.broadcasted_iota(jnp.int32, (PAGE_SIZE, rows2), 0)
    ec = lax.broadcasted_iota(jnp.int32, (PAGE_SIZE, rows2), 1)
    expand = jnp.where(ec // N_KV_HEADS == er, 1.0, 0.0).astype(BF16)
    sel2 = jnp.dot(sel_ref[...].astype(BF16), expand, preferred_element_type=F32)
    hh = lax.broadcasted_iota(jnp.int32, (N_HEADS, rows2), 0)
    cc = lax.broadcasted_iota(jnp.int32, (N_HEADS, rows2), 1)
    head_ok = (hh // GROUP) == (cc % N_KV_HEADS)

    def update(s, v):
        m_prev = m_ref[...]
        m_new = jnp.maximum(m_prev, jnp.max(s, axis=-1, keepdims=True))
        alpha = jnp.exp(m_prev - m_new)
        p = jnp.exp(s - m_new)
        l_ref[...] = alpha * l_ref[...] + jnp.sum(p, axis=-1, keepdims=True)
        acc_ref[...] = alpha * acc_ref[...] + v(p)
        m_ref[...] = m_new

    logits = []
    for u in range(npg):
        s = _dot_nt(q, k_refs[u][...].astype(BF16)) * scale
        logits.append(jnp.where(head_ok & (sel2[u:u + 1, :] > 0.5), s, NEG))

    def weighted_values(p):
        pb = p.astype(BF16)
        out = jnp.zeros((N_HEADS, HEAD_DIM), F32)
        for u in range(npg):
            out = out + jnp.dot(pb[:, u * rows2:(u + 1) * rows2], v_refs[u][...].astype(BF16),
                                preferred_element_type=F32)
        return out

    update(jnp.concatenate(logits, axis=1), weighted_values)

    @pl.when(j == pl.num_programs(1) - 1)
    def _():
        hrow = lax.broadcasted_iota(jnp.int32, (N_HEADS, HEAD_DIM), 0)
        kn = kn_ref[...]
        vn = vn_ref[...]
        k_self = jnp.where(hrow < GROUP, kn[0:1, :], kn[1:2, :])
        v_self = jnp.where(hrow < GROUP, vn[0:1, :], vn[1:2, :])
        s_self = jnp.sum(q_ref[...] * k_self, axis=-1, keepdims=True) * scale
        s_self = jnp.where(selself_ref[0:1, 0:1] > 0.5, s_self, NEG)
        update(s_self, lambda p: p * v_self)
        o_ref[...] = acc_ref[...] / l_ref[...]


def _sample_attn(layer, page_table, cache_k4, cache_v4, sel, selself, q3, kn3, vn3):
    nb, n_pages = page_table.shape
    npg = PAGES_PER_STEP
    rows2 = N_KV_HEADS * PAGE_SIZE
    page_spec = lambda u: pl.BlockSpec(
        (None, None, rows2, HEAD_DIM),
        lambda s, j, pt: (layer, pt[s, j * npg + u], 0, 0))
    per_seq = lambda shape: pl.BlockSpec((None,) + shape, lambda s, j, pt: (s, 0, 0))
    grid_spec = pltpu.PrefetchScalarGridSpec(
        num_scalar_prefetch=1,
        grid=(nb, n_pages // npg),
        in_specs=[page_spec(u) for u in range(npg)] * 2 + [
            pl.BlockSpec((None, npg, PAGE_SIZE), lambda s, j, pt: (s, j, 0)),
            per_seq((1, LANE)),
            per_seq((N_HEADS, HEAD_DIM)),
            per_seq((N_KV_HEADS, HEAD_DIM)),
            per_seq((N_KV_HEADS, HEAD_DIM)),
        ],
        out_specs=per_seq((N_HEADS, HEAD_DIM)),
        scratch_shapes=[
            pltpu.VMEM((N_HEADS, 1), F32),
            pltpu.VMEM((N_HEADS, 1), F32),
            pltpu.VMEM((N_HEADS, HEAD_DIM), F32),
        ],
    )
    return pl.pallas_call(
        _sample_attn_kernel,
        grid_spec=grid_spec,
        out_shape=jax.ShapeDtypeStruct((nb, N_HEADS, HEAD_DIM), F32),
        compiler_params=pltpu.CompilerParams(dimension_semantics=("parallel", "arbitrary")),
        name="sample_attn",
    )(page_table, *([cache_k4] * npg), *([cache_v4] * npg), sel, selself, q3, kn3, vn3)


def _sample_mid_kernel(att_ref, rest_ref, s0_ref, s1_ref, cw_ref, lg_ref, lb_ref, w0_ref, b0_ref,
                       ya_ref, yb_ref, yc_ref, z_ref, vn_ref):
    blk = lambda c: rest_ref[:, c * W_A:(c + 1) * W_A]
    ga, h, bg, cg, gb, u, vc, gc = [blk(c) for c in range(8)]
    ya_ref[...] = att_ref[...] * _silu(ga)
    z = cg * h
    conv = cw_ref[0:1, :] * s0_ref[...] + cw_ref[1:2, :] * s1_ref[...] + cw_ref[2:3, :] * z
    yb_ref[...] = bg * conv * _silu(gb)
    z_ref[...] = z
    mu = jnp.mean(vc, axis=-1, keepdims=True)
    dv = vc - mu
    var = jnp.mean(dv * dv, axis=-1, keepdims=True)
    vn = dv * lax.rsqrt(var + EPS) * lg_ref[...] + lb_ref[...]
    vn_ref[...] = vn
    mixed = w0_ref[...] * vn + b0_ref[...]
    yc_ref[...] = u * mixed * _silu(gc)


def _sample_mid(att, rest, s0, s1, conv_w, ln_g, ln_b, w0, b0):
    nb = att.shape[0]
    shp = jax.ShapeDtypeStruct((nb, W_A), F32)
    return pl.pallas_call(
        _sample_mid_kernel,
        out_shape=[shp] * 5,
        name="sample_mid",
    )(att, rest, s0, s1, conv_w, ln_g.reshape(1, W_C), ln_b.reshape(1, W_C), w0, b0)


def _pack_w_in(w_in):
    d = w_in.shape[0]
    cuts = {}
    off = 0
    for name, size in (("q", W_A), ("k", KV), ("v", KV), ("iq", IDX_HEADS * IDX_DIM),
                       ("ik", IDX_DIM), ("iw", IDX_HEADS)):
        cuts[name] = w_in[:, off:off + size]
        off += size
    z = lambda n: jnp.zeros((d, n), w_in.dtype)
    w_qiq = jnp.concatenate([cuts["q"], cuts["iq"]], axis=1)
    w_kvi = jnp.concatenate([cuts["k"], cuts["v"],
                             cuts["ik"], z(LANE - IDX_DIM),
                             z(LANE - IDX_DIM), cuts["ik"],
                             cuts["iw"], z(LANE - IDX_HEADS)], axis=1)
    w_rest = w_in[:, off:]
    return w_qiq.astype(BF16), w_kvi.astype(BF16), w_rest.astype(BF16)


def kernel(x_prompt, x_sample, cache_k, cache_v, cache_idx_k, state_conv, page_table,
           g_pre, g_post, w_in, conv_w, sgu_ln_g, sgu_ln_b, sgu_w, sgu_b, w_branch, w_out):
    depth = w_in.shape[0]
    batch, seq, d = x_prompt.shape
    nb = x_sample.shape[0]
    n_pool = cache_k.shape[1]
    n_pages = page_table.shape[1]
    topk_s = min(TOPK_MAX, (n_pages * PAGE_SIZE + 1) // 4)
    assert x_sample.shape[1] == 1

    xp = x_prompt.reshape(batch * seq, d)
    xs = x_sample.reshape(nb, d)
    cache_k4 = cache_k.reshape(depth, n_pool, PAGE_SIZE * N_KV_HEADS, HEAD_DIM)
    cache_v4 = cache_v.reshape(depth, n_pool, PAGE_SIZE * N_KV_HEADS, HEAD_DIM)
    cache_ik_t = jnp.swapaxes(cache_idx_k, 2, 3)

    outs = {name: [] for name in ("kp", "vp", "ikp", "ks", "vs", "iks", "cp", "cs", "chs")}
    for l in range(depth):
        w_qiq, w_kvi, w_rest = _pack_w_in(w_in[l])
        w_br = w_branch[l].astype(BF16)
        w_o = w_out[l].astype(BF16)

        (qiq,) = _norm_matmul(xp, g_pre[l], w_qiq, (BF16,), 1024, 1024)
        kvi_f, kvi_b = _norm_matmul(xp, g_pre[l], w_kvi, (F32, BF16), 1024, KVI_W)
        (rest,) = _norm_matmul(xp, g_pre[l], w_rest, (BF16,), 1024, 1024)
        ya = _dsa_prompt(qiq, kvi_b, kvi_f, rest, batch, seq)
        yb, yc, zt = _bc_prompt(rest, conv_w[l], sgu_ln_g[l], sgu_ln_b[l], sgu_w[l], sgu_b[l],
                                batch, seq, min(512, seq))
        mm = _merge(ya, yb, yc, w_br, rest, d, 1024, min(512, d))
        xp = _outproj(mm, w_o, xp, g_post[l], 512)
        outs["kp"].append(kvi_f[:, 0:KV].reshape(batch, seq // PAGE_SIZE, PAGE_SIZE,
                                                 N_KV_HEADS, HEAD_DIM))
        outs["vp"].append(kvi_f[:, KV:2 * KV].reshape(batch, seq // PAGE_SIZE, PAGE_SIZE,
                                                      N_KV_HEADS, HEAD_DIM))
        outs["ikp"].append(kvi_f[:, KVI_IK:KVI_IK + IDX_DIM].reshape(
            batch, seq // PAGE_SIZE, PAGE_SIZE, IDX_DIM))
        outs["cp"].append(zt.reshape(batch, 8, W_B)[:, 8 - (CONV_W - 1):])

        (qiq_s,) = _norm_matmul(xs, g_pre[l], w_qiq, (F32,), 8, 1024)
        (kvi_s,) = _norm_matmul(xs, g_pre[l], w_kvi, (F32,), 8, KVI_W)
        (rest_s,) = _norm_matmul(xs, g_pre[l], w_rest, (F32,), 8, 1024)
        q3 = qiq_s[:, :W_A].reshape(nb, N_HEADS, HEAD_DIM)
        iq3 = qiq_s[:, W_A:].reshape(nb, IDX_HEADS, IDX_DIM)
        kn = kvi_s[:, 0:KV]
        vn = kvi_s[:, KV:2 * KV]
        ikn = kvi_s[:, KVI_IK:KVI_IK + IDX_DIM]
        w3 = kvi_s[:, KVI_IW:KVI_IW + IDX_HEADS].reshape(nb, IDX_HEADS, 1)
        scores = _sample_scores(l, page_table, cache_ik_t, iq3, w3)
        sel, selself = _sample_select(scores, iq3, ikn.reshape(nb, 1, IDX_DIM), w3, topk_s)
        att = _sample_attn(l, page_table, cache_k4, cache_v4, sel, selself, q3,
                           kn.reshape(nb, N_KV_HEADS, HEAD_DIM),
                           vn.reshape(nb, N_KV_HEADS, HEAD_DIM))
        gdim = W_C // C_GROUPS
        w0 = jnp.repeat(sgu_w[l][:, 0, 0], gdim).reshape(1, W_C)
        b0 = jnp.repeat(sgu_b[l][:, 0], gdim).reshape(1, W_C)
        ya_s, yb_s, yc_s, z_s, vn_s = _sample_mid(
            att.reshape(nb, W_A), rest_s, state_conv[l][:, 0], state_conv[l][:, 1],
            conv_w[l], sgu_ln_g[l], sgu_ln_b[l], w0, b0)
        mm_s = _merge(ya_s, yb_s, yc_s, w_br, rest_s, d, 8, min(512, d))
        xs = _outproj(mm_s, w_o, xs, g_post[l], 8)
        outs["ks"].append(kn.reshape(nb, 1, N_KV_HEADS, HEAD_DIM))
        outs["vs"].append(vn.reshape(nb, 1, N_KV_HEADS, HEAD_DIM))
        outs["iks"].append(ikn.reshape(nb, 1, IDX_DIM))
        outs["cs"].append(jnp.stack([state_conv[l][:, 1], z_s], axis=1))
        outs["chs"].append(vn_s.reshape(nb, 1, W_C))

    st = lambda name: jnp.stack(outs[name])
    return (xp.reshape(batch, seq, d), xs.reshape(nb, 1, d),
            st("kp"), st("vp"), st("ikp"), st("ks"), st("vs"), st("iks"),
            st("cp"), st("cs"), st("chs"))
```

```python
import functools

import jax
import jax.numpy as jnp
from jax import lax
from jax.experimental import pallas as pl
from jax.experimental.pallas import tpu as pltpu

N_HEADS = 8
HEAD_DIM = 128
N_KV_HEADS = 2
GROUP = N_HEADS // N_KV_HEADS
W_A = N_HEADS * HEAD_DIM
KV = N_KV_HEADS * HEAD_DIM
IDX_HEADS = 16
IDX_DIM = 64
TOPK_MAX = 256
Q_BLOCK = 128
PAGE_SIZE = 128
W_B = 1024
CONV_W = 3
W_C = 1024
CHUNK = 128
C_GROUPS = 8
EPS = 1e-6
NEG = -1e30
BIG = 3e38

LANE = 128
KEY_CHUNK = 256
PAGES_PER_STEP = 16
INT_MIN = -(2 ** 31)
LOG2_E = 1.4426950408889634
BF16_SUBLANES = 16
VT_ROWS = HEAD_DIM + BF16_SUBLANES
COUNT_ROWS = 64
WT_CHUNK = 256

ROW_Q = 0
ROW_KV = ROW_Q + W_A
ROW_IQ = ROW_KV + 2 * KV
ROW_IK = ROW_IQ + IDX_HEADS * IDX_DIM
ROW_REST = ROW_IK + IDX_DIM + IDX_HEADS

KVI_W = KV + KV + 3 * LANE
KVI_IK = 2 * KV
KVI_IW = 2 * KV + 2 * LANE

F32 = jnp.float32
BF16 = jnp.bfloat16


def _silu(x):
    return x / (1.0 + jnp.exp(-x))


def _sigmoid(x):
    return 1.0 / (1.0 + jnp.exp(-x))


def _dot_nt(a, b):
    return lax.dot_general(a, b, (((1,), (1,)), ((), ())), preferred_element_type=F32)


def _ordered_key(x):
    bits = pltpu.bitcast(x, jnp.int32)
    return jnp.where(bits < 0, bits ^ jnp.int32(0x7FFFFFFF), bits)


def _key_to_float(key):
    bits = jnp.where(key < 0, key ^ jnp.int32(0x7FFFFFFF), key)
    return pltpu.bitcast(bits, F32)


def _kth_largest_key(count_ge, shape, k):
    zero = jnp.zeros(shape, jnp.int32)
    t = jnp.where(count_ge(zero) >= k, zero, jnp.full(shape, INT_MIN, jnp.int32))

    def body(it, t):
        cand = t | jnp.left_shift(jnp.int32(1), jnp.int32(30) - it)
        return jnp.where(count_ge(cand) >= k, cand, t)

    return lax.fori_loop(0, 31, body, t)


def _rmsnorm_kernel(x_ref, g_ref, o_ref):
    x = x_ref[...]
    ms = jnp.mean(x * x, axis=-1, keepdims=True)
    o_ref[...] = (x * lax.rsqrt(ms + EPS) * g_ref[...]).astype(o_ref.dtype)


def _rmsnorm(x, g, out_dtype, tm):
    m, d = x.shape
    tm = min(tm, m)
    return pl.pallas_call(
        _rmsnorm_kernel,
        grid=(m // tm,),
        in_specs=[pl.BlockSpec((tm, d), lambda i: (i, 0)), pl.BlockSpec((1, d), lambda i: (0, 0))],
        out_specs=pl.BlockSpec((tm, d), lambda i: (i, 0)),
        out_shape=jax.ShapeDtypeStruct((m, d), out_dtype),
        compiler_params=pltpu.CompilerParams(dimension_semantics=("parallel",)),
        name="rmsnorm",
    )(x, g.reshape(1, d))


def _load_transposed(wt_ref, wb_ref, col0=0):
    n, d = wt_ref.shape
    for kb in range(d // WT_CHUNK):
        ks = slice(kb * WT_CHUNK, (kb + 1) * WT_CHUNK)
        wb_ref[ks, col0:col0 + n] = wt_ref[:, ks].T.astype(BF16)


def _proj_kernel(xn_ref, xs_ref, wt_ref, o_ref, os_ref, wb_ref):
    @pl.when(pl.program_id(1) == 0)
    def _():
        _load_transposed(wt_ref, wb_ref)
        os_ref[...] = jnp.dot(xs_ref[...].astype(BF16), wb_ref[...], preferred_element_type=F32)

    o_ref[...] = jnp.dot(xn_ref[...], wb_ref[...],
                         preferred_element_type=F32).astype(o_ref.dtype)


def _proj(xn, xs_n, wt, layer, row_of_tile, n_tiles, tn, tm):
    m, d = xn.shape
    nbs = xs_n.shape[0]
    tm = min(tm, m)
    return pl.pallas_call(
        _proj_kernel,
        grid=(n_tiles, m // tm),
        in_specs=[
            pl.BlockSpec((tm, d), lambda j, i: (i, 0)),
            pl.BlockSpec((nbs, d), lambda j, i: (0, 0)),
            pl.BlockSpec((None, pl.Element(tn), pl.Element(d)),
                         lambda j, i: (layer, pl.multiple_of(row_of_tile(j), 8), 0)),
        ],
        out_specs=[pl.BlockSpec((tm, tn), lambda j, i: (i, j)),
                   pl.BlockSpec((nbs, tn), lambda j, i: (0, j))],
        out_shape=[jax.ShapeDtypeStruct((m, n_tiles * tn), BF16),
                   jax.ShapeDtypeStruct((nbs, n_tiles * tn), F32)],
        scratch_shapes=[pltpu.VMEM((d, tn), BF16)],
        compiler_params=pltpu.CompilerParams(
            dimension_semantics=("arbitrary", "arbitrary"),
            vmem_limit_bytes=56 * 1024 * 1024),
        name="proj",
    )(xn, xs_n, wt)


def _proj_kvi_kernel(xn_ref, xs_ref, wkv_ref, wix_ref, of_ref, ob_ref, os_ref, wb_ref):
    @pl.when(pl.program_id(0) == 0)
    def _():
        _load_transposed(wkv_ref, wb_ref)
        d = wb_ref.shape[0]
        lane = lax.broadcasted_iota(jnp.int32, (WT_CHUNK, LANE), 1)
        for kb in range(d // WT_CHUNK):
            ks = slice(kb * WT_CHUNK, (kb + 1) * WT_CHUNK)
            t = wix_ref[:, ks].T
            r = pltpu.roll(t, LANE // 2, 1)
            wb_ref[ks, KVI_IK:KVI_IK + LANE] = jnp.where(lane < IDX_DIM, t, 0.0).astype(BF16)
            wb_ref[ks, KVI_IK + LANE:KVI_IW] = jnp.where(lane >= IDX_DIM, r, 0.0).astype(BF16)
            wb_ref[ks, KVI_IW:KVI_W] = jnp.where(lane < IDX_HEADS, r, 0.0).astype(BF16)
        os_ref[...] = jnp.dot(xs_ref[...].astype(BF16), wb_ref[...], preferred_element_type=F32)

    acc = jnp.dot(xn_ref[...], wb_ref[...], preferred_element_type=F32)
    of_ref[...] = acc
    ob_ref[...] = acc.astype(BF16)


def _proj_kvi(xn, xs_n, wt, layer, kv_row, ik_row, tm):
    m, d = xn.shape
    nbs = xs_n.shape[0]
    tm = min(tm, m)
    return pl.pallas_call(
        _proj_kvi_kernel,
        grid=(m // tm,),
        in_specs=[
            pl.BlockSpec((tm, d), lambda i: (i, 0)),
            pl.BlockSpec((nbs, d), lambda i: (0, 0)),
            pl.BlockSpec((None, pl.Element(2 * KV), pl.Element(d)), lambda i: (layer, kv_row, 0)),
            pl.BlockSpec((None, pl.Element(LANE), pl.Element(d)), lambda i: (layer, ik_row, 0)),
        ],
        out_specs=[pl.BlockSpec((tm, KVI_W), lambda i: (i, 0)),
                   pl.BlockSpec((tm, KVI_W), lambda i: (i, 0)),
                   pl.BlockSpec((nbs, KVI_W), lambda i: (0, 0))],
        out_shape=[jax.ShapeDtypeStruct((m, KVI_W), F32),
                   jax.ShapeDtypeStruct((m, KVI_W), BF16),
                   jax.ShapeDtypeStruct((nbs, KVI_W), F32)],
        scratch_shapes=[pltpu.VMEM((d, KVI_W), BF16)],
        compiler_params=pltpu.CompilerParams(
            dimension_semantics=("arbitrary",),
            vmem_limit_bytes=56 * 1024 * 1024),
        name="proj_kvi",
    )(xn, xs_n, wt, wt)


def _transpose_bf16(x):
    return x.astype(F32).T.astype(BF16)


def _dsa_prompt_kernel(q_ref, iq_ref, w_ref, k_ref, v_ref, ik_ref, ga_ref, o_ref,
                       st_ref, key_ref, iqt_ref, qgt_ref, wt_ref, vt_ref,
                       m0_ref, m1_ref, acc0_ref, acc1_ref, sba_ref, sbb_ref, *, topk):
    i = pl.program_id(1)
    qb = q_ref.shape[0]
    seq = k_ref.shape[0]
    kc = KEY_CHUNK
    nk = (i + 1) * qb
    nch = (nk + kc - 1) // kc
    qpos = i * qb + lax.broadcasted_iota(jnp.int32, (kc, qb), 1)
    krow = lax.broadcasted_iota(jnp.int32, (kc, qb), 0)

    @pl.when(i == 0)
    def _():
        def body(c, carry):
            off = pl.multiple_of(c * kc, kc)
            vt = _transpose_bf16(v_ref[pl.ds(off, kc), :])
            for g in range(N_KV_HEADS):
                vt_ref[g, 0:HEAD_DIM, pl.ds(off, kc)] = vt[g * HEAD_DIM:(g + 1) * HEAD_DIM]
                vt_ref[g, HEAD_DIM:VT_ROWS, pl.ds(off, kc)] = jnp.ones(
                    (VT_ROWS - HEAD_DIM, kc), BF16)
            return carry
        lax.fori_loop(0, seq // kc, body, 0)

    n_tiles = IDX_HEADS // 2
    for j in range(n_tiles):
        iqt_ref[:, j * qb:(j + 1) * qb] = _transpose_bf16(iq_ref[:, j * LANE:(j + 1) * LANE])
    for g in range(N_KV_HEADS):
        for hl in range(GROUP):
            h = g * GROUP + hl
            qgt_ref[g, :, hl * qb:(hl + 1) * qb] = _transpose_bf16(
                q_ref[:, h * HEAD_DIM:(h + 1) * HEAD_DIM])
    wt_ref[...] = (w_ref[...] * ((IDX_HEADS ** -0.5) * (IDX_DIM ** -0.5))).T

    def score_body(c, carry):
        off = pl.multiple_of(c * kc, kc)
        acc = jnp.zeros((kc, qb), F32)
        for half in range(2):
            res = jnp.dot(ik_ref[pl.ds(off, kc), half * LANE:(half + 1) * LANE], iqt_ref[...],
                          preferred_element_type=F32)
            for j in range(n_tiles):
                h = 2 * j + half
                acc = acc + jnp.maximum(res[:, j * qb:(j + 1) * qb], 0.0) * wt_ref[h:h + 1, :]
        sc = jnp.where(off + krow <= qpos, acc, NEG)
        st_ref[pl.ds(off, kc), :] = sc
        key_ref[pl.ds(off, kc), :] = _ordered_key(sc)
        return carry

    lax.fori_loop(0, nch, score_body, 0)

    def count_ge(cand):
        def body(c, acc):
            off = pl.multiple_of(c * kc, kc)
            ge = jnp.where(key_ref[pl.ds(off, kc), :] >= cand, 1.0, 0.0)
            return acc + jnp.sum(ge.reshape(kc // COUNT_ROWS, COUNT_ROWS, qb), axis=0)
        acc = lax.fori_loop(0, nch, body, jnp.zeros((COUNT_ROWS, qb), F32))
        return jnp.sum(acc, axis=0, keepdims=True)

    thr = _key_to_float(_kth_largest_key(count_ge, (1, qb), float(topk)))

    def bias_body(c, carry):
        off = pl.multiple_of(c * kc, kc)
        sel = (st_ref[pl.ds(off, kc), :] >= thr) & (off + krow <= qpos)
        st_ref[pl.ds(off, kc), :] = jnp.where(sel, BIG, NEG)
        return carry

    lax.fori_loop(0, nch, bias_body, 0)

    exp2_scale = (HEAD_DIM ** -0.5) * LOG2_E
    m_refs = (m0_ref, m1_ref)
    acc_refs = (acc0_ref, acc1_ref)
    for g in range(N_KV_HEADS):
        m_refs[g][...] = jnp.full(m_refs[g].shape, -jnp.inf, F32)
        acc_refs[g][...] = jnp.zeros(acc_refs[g].shape, F32)

    def logits(c, dst_ref):
        off = pl.multiple_of(jnp.minimum(c, nch - 1) * kc, kc)
        for g in range(N_KV_HEADS):
            dst_ref[g] = jnp.dot(
                k_ref[pl.ds(off, kc), g * HEAD_DIM:(g + 1) * HEAD_DIM], qgt_ref[g],
                preferred_element_type=F32)

    def softmax_pv(c, src_ref):
        off = pl.multiple_of(jnp.minimum(c, nch - 1) * kc, kc)
        bound = jnp.where(c < nch, st_ref[pl.ds(off, kc), :], NEG)
        bound = jnp.concatenate([bound] * GROUP, axis=1)
        for g in range(N_KV_HEADS):
            s = jnp.minimum(src_ref[g], bound)
            m_prev = m_refs[g][...]
            m_new = jnp.maximum(m_prev, jnp.max(s, axis=0, keepdims=True))
            alpha = jnp.exp2((m_prev - m_new) * exp2_scale)
            p = jnp.exp2((s - m_new) * exp2_scale)
            acc_refs[g][...] = alpha * acc_refs[g][...] + jnp.dot(
                vt_ref[g, :, pl.ds(off, kc)], p.astype(BF16), preferred_element_type=F32)
            m_refs[g][...] = m_new

    logits(0, sba_ref)

    def attn_body(c2, carry):
        c = 2 * c2
        logits(c + 1, sbb_ref)
        softmax_pv(c, sba_ref)
        logits(c + 2, sba_ref)
        softmax_pv(c + 1, sbb_ref)
        return carry

    lax.fori_loop(0, (nch + 1) // 2, attn_body, 0)
    for g in range(N_KV_HEADS):
        acc = acc_refs[g][...]
        o_t = acc[0:HEAD_DIM] / acc[HEAD_DIM:HEAD_DIM + 1]
        for hl in range(GROUP):
            h = g * GROUP + hl
            cols = slice(h * HEAD_DIM, (h + 1) * HEAD_DIM)
            gate = _silu(ga_ref[:, cols].astype(F32))
            o_ref[:, cols] = (o_t[:, hl * qb:(hl + 1) * qb].T * gate).astype(o_ref.dtype)


def _dsa_prompt(qiq, kvi_b, kvi_f, rest, batch, seq):
    m = qiq.shape[0]
    qb = Q_BLOCK
    nq = seq // qb
    topk = min(TOPK_MAX, seq // 4)
    assert seq % KEY_CHUNK == 0 and topk <= KEY_CHUNK
    row = lambda b, i: b * nq + i
    return pl.pallas_call(
        functools.partial(_dsa_prompt_kernel, topk=topk),
        grid=(batch, nq),
        in_specs=[
            pl.BlockSpec((qb, W_A), lambda b, i: (row(b, i), 0)),
            pl.BlockSpec((qb, IDX_HEADS * IDX_DIM), lambda b, i: (row(b, i), 1)),
            pl.BlockSpec((qb, LANE), lambda b, i: (row(b, i), KVI_IW // LANE)),
            pl.BlockSpec((seq, KV), lambda b, i: (b, 0)),
            pl.BlockSpec((seq, KV), lambda b, i: (b, 1)),
            pl.BlockSpec((seq, 2 * LANE), lambda b, i: (b, KVI_IK // (2 * LANE))),
            pl.BlockSpec((qb, W_A), lambda b, i: (row(b, i), 0)),
        ],
        out_specs=pl.BlockSpec((qb, W_A), lambda b, i: (row(b, i), 0)),
        out_shape=jax.ShapeDtypeStruct((m, W_A), BF16),
        scratch_shapes=[
            pltpu.VMEM((seq, qb), F32),
            pltpu.VMEM((seq, qb), jnp.int32),
            pltpu.VMEM((LANE, IDX_HEADS // 2 * qb), BF16),
            pltpu.VMEM((N_KV_HEADS, HEAD_DIM, GROUP * qb), BF16),
            pltpu.VMEM((LANE, qb), F32),
            pltpu.VMEM((N_KV_HEADS, VT_ROWS, seq), BF16),
            pltpu.VMEM((1, GROUP * qb), F32),
            pltpu.VMEM((1, GROUP * qb), F32),
            pltpu.VMEM((VT_ROWS, GROUP * qb), F32),
            pltpu.VMEM((VT_ROWS, GROUP * qb), F32),
            pltpu.VMEM((N_KV_HEADS, KEY_CHUNK, GROUP * qb), F32),
            pltpu.VMEM((N_KV_HEADS, KEY_CHUNK, GROUP * qb), F32),
        ],
        compiler_params=pltpu.CompilerParams(
            dimension_semantics=("parallel", "arbitrary"),
            vmem_limit_bytes=56 * 1024 * 1024),
        name="dsa_prompt",
    )(qiq, qiq, kvi_f, kvi_b, kvi_b, kvi_b, rest)


PREV_ROWS = 16


def _bc_prompt_kernel(h_ref, bg_ref, cg_ref, gb_ref, u_ref, vc_ref, gc_ref, ph_ref, pcg_ref,
                      cw_ref, lg_ref, lb_ref, ws_ref, bst_ref,
                      yb_ref, yc_ref, zt_ref, zs_ref):
    tm = h_ref.shape[0]
    j = pl.program_id(1)
    z = cg_ref[...].astype(F32) * h_ref[...].astype(F32)
    pz = pcg_ref[...].astype(F32) * ph_ref[...].astype(F32)
    pz = jnp.where(j == 0, 0.0, pz)
    zs_ref[0:PREV_ROWS, :] = pz
    zs_ref[PREV_ROWS:PREV_ROWS + tm, :] = z
    z1 = zs_ref[PREV_ROWS - 1:PREV_ROWS - 1 + tm, :]
    z2 = zs_ref[PREV_ROWS - 2:PREV_ROWS - 2 + tm, :]
    conv = cw_ref[0:1, :] * z2 + cw_ref[1:2, :] * z1 + cw_ref[2:3, :] * z
    yb = bg_ref[...].astype(F32) * conv * _silu(gb_ref[...].astype(F32))
    yb_ref[...] = yb.astype(yb_ref.dtype)
    zt_ref[...] = z[tm - 8:tm]

    vc = vc_ref[...].astype(F32)
    mu = jnp.mean(vc, axis=-1, keepdims=True)
    dv = vc - mu
    var = jnp.mean(dv * dv, axis=-1, keepdims=True)
    vn = (dv * lax.rsqrt(var + EPS) * lg_ref[...] + lb_ref[...]).astype(BF16)
    rr = lax.broadcasted_iota(jnp.int32, (CHUNK, CHUNK), 0)
    cc = lax.broadcasted_iota(jnp.int32, (CHUNK, CHUNK), 1)
    gdim = W_C // C_GROUPS
    for g in range(C_GROUPS):
        wg = jnp.where(rr >= cc, ws_ref[g], 0.0).astype(BF16)
        bias = bst_ref[:, g:g + 1]
        cols = slice(g * gdim, (g + 1) * gdim)
        for c in range(tm // CHUNK):
            rows = slice(c * CHUNK, (c + 1) * CHUNK)
            mixed = jnp.dot(wg, vn[rows, cols], preferred_element_type=F32) + bias
            yc = u_ref[rows, cols].astype(F32) * mixed * _silu(gc_ref[rows, cols].astype(F32))
            yc_ref[rows, cols] = yc.astype(yc_ref.dtype)


def _bc_prompt(rest, conv_w, ln_g, ln_b, ws, bs, batch, seq, tm):
    m = rest.shape[0]
    nt = seq // tm
    row = lambda b, j: b * nt + j
    prev = lambda b, j: jnp.maximum((b * seq + j * tm) // PREV_ROWS - 1, 0)
    col = lambda c: (lambda b, j: (row(b, j), c))
    pcol = lambda c: (lambda b, j: (prev(b, j), c))
    const2 = lambda b, j: (0, 0)
    yb, yc, zt = pl.pallas_call(
        _bc_prompt_kernel,
        grid=(batch, nt),
        in_specs=[pl.BlockSpec((tm, W_B), col(c)) for c in range(1, 8)] + [
            pl.BlockSpec((PREV_ROWS, W_B), pcol(1)),
            pl.BlockSpec((PREV_ROWS, W_B), pcol(3)),
            pl.BlockSpec((CONV_W, W_B), const2),
            pl.BlockSpec((1, W_C), const2),
            pl.BlockSpec((1, W_C), const2),
            pl.BlockSpec((C_GROUPS, CHUNK, CHUNK), lambda b, j: (0, 0, 0)),
            pl.BlockSpec((CHUNK, C_GROUPS), const2),
        ],
        out_specs=[
            pl.BlockSpec((tm, W_B), lambda b, j: (row(b, j), 0)),
            pl.BlockSpec((tm, W_C), lambda b, j: (row(b, j), 0)),
            pl.BlockSpec((8, W_B), lambda b, j: (b, 0)),
        ],
        out_shape=[
            jax.ShapeDtypeStruct((m, W_B), BF16),
            jax.ShapeDtypeStruct((m, W_C), BF16),
            jax.ShapeDtypeStruct((batch * 8, W_B), F32),
        ],
        scratch_shapes=[pltpu.VMEM((tm + PREV_ROWS, W_B), F32)],
        compiler_params=pltpu.CompilerParams(
            dimension_semantics=("parallel", "arbitrary"),
            vmem_limit_bytes=56 * 1024 * 1024),
        name="bc_prompt",
    )(*([rest] * 7), rest, rest, conv_w, ln_g.reshape(1, W_C), ln_b.reshape(1, W_C), ws, bs.T)
    return yb, yc, zt


N_BRANCH = 3


def _merge_kernel(*refs):
    y_refs, ys_refs = refs[0:3], refs[3:6]
    w_refs = refs[6:9]
    r_refs, rs_refs = refs[9:12], refs[12:15]
    o_ref, os_ref, wbf_ref = refs[15:18]

    def merged(ys, rs):
        out = None
        for b in range(N_BRANCH):
            proj = jnp.dot(ys[b][...].astype(BF16), wbf_ref[b], preferred_element_type=F32)
            term = _sigmoid(rs[b][...].astype(F32)) * proj
            out = term if out is None else out + term
        return out

    @pl.when(pl.program_id(1) == 0)
    def _():
        for b in range(N_BRANCH):
            wbf_ref[b] = w_refs[b][...].astype(BF16)
        os_ref[...] = merged(ys_refs, rs_refs)

    o_ref[...] = merged(y_refs, r_refs).astype(o_ref.dtype)


def _merge(ys, ys_s, w_branch, layer, rest, rest_s, d, tm, tn):
    m = ys[0].shape[0]
    nbs = ys_s[0].shape[0]
    tm = min(tm, m)
    nb = d // tn
    r0 = 8 * W_A // tn
    y_spec = pl.BlockSpec((tm, W_A), lambda j, i: (i, 0))
    ys_spec = pl.BlockSpec((nbs, W_A), lambda j, i: (0, 0))
    w_spec = lambda b: pl.BlockSpec((None, W_A, tn), lambda j, i: (layer, b, j))
    r_spec = lambda b: pl.BlockSpec((tm, tn), lambda j, i: (i, r0 + b * nb + j))
    rs_spec = lambda b: pl.BlockSpec((nbs, tn), lambda j, i: (0, r0 + b * nb + j))
    rng = range(N_BRANCH)
    return pl.pallas_call(
        _merge_kernel,
        grid=(nb, m // tm),
        in_specs=([y_spec] * 3 + [ys_spec] * 3 + [w_spec(b) for b in rng]
                  + [r_spec(b) for b in rng] + [rs_spec(b) for b in rng]),
        out_specs=[pl.BlockSpec((tm, tn), lambda j, i: (i, j)),
                   pl.BlockSpec((nbs, tn), lambda j, i: (0, j))],
        out_shape=[jax.ShapeDtypeStruct((m, d), BF16), jax.ShapeDtypeStruct((nbs, d), F32)],
        scratch_shapes=[pltpu.VMEM((N_BRANCH, W_A, tn), BF16)],
        compiler_params=pltpu.CompilerParams(
            dimension_semantics=("arbitrary", "arbitrary"),
            vmem_limit_bytes=56 * 1024 * 1024),
        name="merge",
    )(*ys, *ys_s, *([w_branch] * 3), *([rest] * 3), *([rest_s] * 3))


def _outproj_kernel(m_ref, ms_ref, w_ref, x_ref, xs_ref, g_ref, o_ref, os_ref, wbf_ref):
    def residual(m, x):
        out = jnp.dot(m.astype(BF16), wbf_ref[...], preferred_element_type=F32)
        ms = jnp.mean(out * out, axis=-1, keepdims=True)
        return x + out * lax.rsqrt(ms + EPS) * g_ref[...]

    @pl.when(pl.program_id(0) == 0)
    def _():
        wbf_ref[...] = w_ref[...].astype(BF16)
        os_ref[...] = residual(ms_ref[...], xs_ref[...])

    o_ref[...] = residual(m_ref[...], x_ref[...])


def _outproj(mm, mm_s, w_out, layer, x, xs, g, tm):
    m, d = x.shape
    nbs = xs.shape[0]
    tm = min(tm, m)
    row_spec = pl.BlockSpec((tm, d), lambda i: (i, 0))
    s_spec = pl.BlockSpec((nbs, d), lambda i: (0, 0))
    return pl.pallas_call(
        _outproj_kernel,
        grid=(m // tm,),
        in_specs=[
            row_spec, s_spec,
            pl.BlockSpec((None, d, d), lambda i: (layer, 0, 0), pipeline_mode=pl.Buffered(1)),
            row_spec, s_spec,
            pl.BlockSpec((1, d), lambda i: (0, 0)),
        ],
        out_specs=[row_spec, s_spec],
        out_shape=[jax.ShapeDtypeStruct((m, d), F32), jax.ShapeDtypeStruct((nbs, d), F32)],
        scratch_shapes=[pltpu.VMEM((d, d), BF16)],
        compiler_params=pltpu.CompilerParams(
            dimension_semantics=("arbitrary",),
            vmem_limit_bytes=56 * 1024 * 1024),
        name="outproj",
    )(mm, mm_s, w_out, x, xs, g.reshape(1, d))


def _sample_scores_kernel(pt_ref, *refs):
    del pt_ref
    npg = PAGES_PER_STEP
    page_refs = refs[:npg]
    iq_ref, w_ref, o_ref = refs[npg:]
    iq = iq_ref[...].astype(BF16)
    w = w_ref[...] * ((IDX_HEADS ** -0.5) * (IDX_DIM ** -0.5))
    for u in range(npg):
        s = jnp.dot(iq, page_refs[u][...].astype(BF16),
                    preferred_element_type=F32)
        o_ref[u:u + 1, :] = jnp.sum(jnp.maximum(s, 0.0) * w, axis=0, keepdims=True)


def _sample_scores(layer, page_table, cache_ik_t, iq3, w3):
    nb, n_pages = page_table.shape
    npg = PAGES_PER_STEP
    assert n_pages % npg == 0
    page_spec = lambda u: pl.BlockSpec(
        (None, None, IDX_DIM, PAGE_SIZE),
        lambda s, j, pt: (layer, pt[s, j * npg + u], 0, 0))
    grid_spec = pltpu.PrefetchScalarGridSpec(
        num_scalar_prefetch=1,
        grid=(nb, n_pages // npg),
        in_specs=[page_spec(u) for u in range(npg)] + [
            pl.BlockSpec((None, IDX_HEADS, IDX_DIM), lambda s, j, pt: (s, 0, 0)),
            pl.BlockSpec((None, IDX_HEADS, 1), lambda s, j, pt: (s, 0, 0)),
        ],
        out_specs=pl.BlockSpec((None, npg, PAGE_SIZE), lambda s, j, pt: (s, j, 0)),
    )
    return pl.pallas_call(
        _sample_scores_kernel,
        grid_spec=grid_spec,
        out_shape=jax.ShapeDtypeStruct((nb, n_pages, PAGE_SIZE), F32),
        compiler_params=pltpu.CompilerParams(dimension_semantics=("parallel", "arbitrary")),
        name="sample_scores",
    )(page_table, *([cache_ik_t] * npg), iq3, w3)


def _sample_select_kernel(sc_ref, iq_ref, ikn_ref, w_ref, sel_ref, selself_ref, *, topk):
    nb = sc_ref.shape[0]
    w = w_ref[...] * ((IDX_HEADS ** -0.5) * (IDX_DIM ** -0.5))
    s_self = jnp.sum(iq_ref[...] * ikn_ref[...], axis=-1, keepdims=True)
    sc_self = jnp.sum(jnp.maximum(s_self, 0.0) * w, axis=1, keepdims=True)
    sc = sc_ref[...]
    key = _ordered_key(sc)
    key_self = _ordered_key(sc_self)

    def count_ge(cand):
        ge = jnp.where(key >= cand, 1.0, 0.0)
        cnt = jnp.sum(jnp.sum(ge, axis=2, keepdims=True), axis=1, keepdims=True)
        return cnt + jnp.where(key_self >= cand, 1.0, 0.0)

    thr = _key_to_float(_kth_largest_key(count_ge, (nb, 1, 1), float(topk)))
    sel_ref[...] = jnp.where(sc >= thr, 1.0, 0.0)
    selself_ref[...] = jnp.broadcast_to(jnp.where(sc_self >= thr, 1.0, 0.0), selself_ref.shape)


def _sample_select(scores, iq3, ikn3, w3, topk):
    nb, n_pages, _ = scores.shape
    return pl.pallas_call(
        functools.partial(_sample_select_kernel, topk=topk),
        out_shape=[jax.ShapeDtypeStruct((nb, n_pages, PAGE_SIZE), F32),
                   jax.ShapeDtypeStruct((nb, 1, LANE), F32)],
        name="sample_select",
    )(scores, iq3, ikn3, w3)


def _sample_attn_kernel(pt_ref, *refs):
    del pt_ref
    npg = PAGES_PER_STEP
    k_refs = refs[:npg]
    v_refs = refs[npg:2 * npg]
    sel_ref, selself_ref, q_ref, kn_ref, vn_ref, o_ref, m_ref, l_ref, acc_ref = refs[2 * npg:]
    j = pl.program_id(1)
    scale = HEAD_DIM ** -0.5
    rows2 = N_KV_HEADS * PAGE_SIZE

    @pl.when(j == 0)
    def _():
        m_ref[...] = jnp.full(m_ref.shape, -jnp.inf, F32)
        l_ref[...] = jnp.zeros(l_ref.shape, F32)
        acc_ref[...] = jnp.zeros(acc_ref.shape, F32)

    q = q_ref[...].astype(BF16)
    er = lax.broadcasted_iota(jnp.int32, (PAGE_SIZE, rows2), 0)
    ec = lax.broadcasted_iota(jnp.int32, (PAGE_SIZE, rows2), 1)
    expand = jnp.where(ec // N_KV_HEADS == er, 1.0, 0.0).astype(BF16)
    sel2 = jnp.dot(sel_ref[...].astype(BF16), expand, preferred_element_type=F32)
    hh = lax.broadcasted_iota(jnp.int32, (N_HEADS, rows2), 0)
    cc = lax.broadcasted_iota(jnp.int32, (N_HEADS, rows2), 1)
    head_ok = (hh // GROUP) == (cc % N_KV_HEADS)

    def update(s, v):
        m_prev = m_ref[...]
        m_new = jnp.maximum(m_prev, jnp.max(s, axis=-1, keepdims=True))
        alpha = jnp.exp(m_prev - m_new)
        p = jnp.exp(s - m_new)
        l_ref[...] = alpha * l_ref[...] + jnp.sum(p, axis=-1, keepdims=True)
        acc_ref[...] = alpha * acc_ref[...] + v(p)
        m_ref[...] = m_new

    logits = []
    for u in range(npg):
        s = _dot_nt(q, k_refs[u][...].astype(BF16)) * scale
        logits.append(jnp.where(head_ok & (sel2[u:u + 1, :] > 0.5), s, NEG))

    def weighted_values(p):
        pb = p.astype(BF16)
        out = jnp.zeros((N_HEADS, HEAD_DIM), F32)
        for u in range(npg):
            out = out + jnp.dot(pb[:, u * rows2:(u + 1) * rows2], v_refs[u][...].astype(BF16),
                                preferred_element_type=F32)
        return out

    update(jnp.concatenate(logits, axis=1), weighted_values)

    @pl.when(j == pl.num_programs(1) - 1)
    def _():
        hrow = lax.broadcasted_iota(jnp.int32, (N_HEADS, HEAD_DIM), 0)
        kn = kn_ref[...]
        vn = vn_ref[...]
        k_self = jnp.where(hrow < GROUP, kn[0:1, :], kn[1:2, :])
        v_self = jnp.where(hrow < GROUP, vn[0:1, :], vn[1:2, :])
        s_self = jnp.sum(q_ref[...] * k_self, axis=-1, keepdims=True) * scale
        s_self = jnp.where(selself_ref[0:1, 0:1] > 0.5, s_self, NEG)
        update(s_self, lambda p: p * v_self)
        o_ref[...] = acc_ref[...] / l_ref[...]


def _sample_attn(layer, page_table, cache_k4, cache_v4, sel, selself, q3, kn3, vn3):
    nb, n_pages = page_table.shape
    npg = PAGES_PER_STEP
    rows2 = N_KV_HEADS * PAGE_SIZE
    page_spec = lambda u: pl.BlockSpec(
        (None, None, rows2, HEAD_DIM),
        lambda s, j, pt: (layer, pt[s, j * npg + u], 0, 0))
    per_seq = lambda shape: pl.BlockSpec((None,) + shape, lambda s, j, pt: (s, 0, 0))
    grid_spec = pltpu.PrefetchScalarGridSpec(
        num_scalar_prefetch=1,
        grid=(nb, n_pages // npg),
        in_specs=[page_spec(u) for u in range(npg)] * 2 + [
            pl.BlockSpec((None, npg, PAGE_SIZE), lambda s, j, pt: (s, j, 0)),
            per_seq((1, LANE)),
            per_seq((N_HEADS, HEAD_DIM)),
            per_seq((N_KV_HEADS, HEAD_DIM)),
            per_seq((N_KV_HEADS, HEAD_DIM)),
        ],
        out_specs=per_seq((N_HEADS, HEAD_DIM)),
        scratch_shapes=[
            pltpu.VMEM((N_HEADS, 1), F32),
            pltpu.VMEM((N_HEADS, 1), F32),
            pltpu.VMEM((N_HEADS, HEAD_DIM), F32),
        ],
    )
    return pl.pallas_call(
        _sample_attn_kernel,
        grid_spec=grid_spec,
        out_shape=jax.ShapeDtypeStruct((nb, N_HEADS, HEAD_DIM), F32),
        compiler_params=pltpu.CompilerParams(dimension_semantics=("parallel", "arbitrary")),
        name="sample_attn",
    )(page_table, *([cache_k4] * npg), *([cache_v4] * npg), sel, selself, q3, kn3, vn3)


def _sample_mid_kernel(att_ref, rest_ref, s0_ref, s1_ref, cw_ref, lg_ref, lb_ref, w0_ref, b0_ref,
                       ya_ref, yb_ref, yc_ref, z_ref, vn_ref):
    blk = lambda c: rest_ref[:, c * W_A:(c + 1) * W_A]
    ga, h, bg, cg, gb, u, vc, gc = [blk(c) for c in range(8)]
    ya_ref[...] = att_ref[...] * _silu(ga)
    z = cg * h
    conv = cw_ref[0:1, :] * s0_ref[...] + cw_ref[1:2, :] * s1_ref[...] + cw_ref[2:3, :] * z
    yb_ref[...] = bg * conv * _silu(gb)
    z_ref[...] = z
    mu = jnp.mean(vc, axis=-1, keepdims=True)
    dv = vc - mu
    var = jnp.mean(dv * dv, axis=-1, keepdims=True)
    vn = dv * lax.rsqrt(var + EPS) * lg_ref[...] + lb_ref[...]
    vn_ref[...] = vn
    mixed = w0_ref[...] * vn + b0_ref[...]
    yc_ref[...] = u * mixed * _silu(gc)


def _sample_mid(att, rest, s0, s1, conv_w, ln_g, ln_b, w0, b0):
    nb = att.shape[0]
    shp = jax.ShapeDtypeStruct((nb, W_A), F32)
    return pl.pallas_call(
        _sample_mid_kernel,
        out_shape=[shp] * 5,
        name="sample_mid",
    )(att, rest, s0, s1, conv_w, ln_g.reshape(1, W_C), ln_b.reshape(1, W_C), w0, b0)


def kernel(x_prompt, x_sample, cache_k, cache_v, cache_idx_k, state_conv, page_table,
           g_pre, g_post, w_in, conv_w, sgu_ln_g, sgu_ln_b, sgu_w, sgu_b, w_branch, w_out):
    depth = w_in.shape[0]
    batch, seq, d = x_prompt.shape
    nb = x_sample.shape[0]
    n_pool = cache_k.shape[1]
    n_pages = page_table.shape[1]
    topk_s = min(TOPK_MAX, (n_pages * PAGE_SIZE + 1) // 4)
    assert x_sample.shape[1] == 1

    xp = x_prompt.reshape(batch * seq, d)
    xs = x_sample.reshape(nb, d)
    cache_k4 = cache_k.reshape(depth, n_pool, PAGE_SIZE * N_KV_HEADS, HEAD_DIM)
    cache_v4 = cache_v.reshape(depth, n_pool, PAGE_SIZE * N_KV_HEADS, HEAD_DIM)
    cache_ik_t = jnp.swapaxes(cache_idx_k, 2, 3)

    wt = jnp.swapaxes(w_in, 1, 2)
    n_rest = wt.shape[1] - ROW_REST
    tn_rest = 1024
    while n_rest % tn_rest:
        tn_rest //= 2
    assert tn_rest % LANE == 0

    outs = {name: [] for name in ("kp", "vp", "ikp", "ks", "vs", "iks", "cp", "cs", "chs")}
    for l in range(depth):
        xn = _rmsnorm(xp, g_pre[l], BF16, 1024)
        xs_n = _rmsnorm(xs, g_pre[l], F32, 8)
        qiq, qiq_s = _proj(xn, xs_n, wt, l, lambda j: jnp.where(j == 0, ROW_Q, ROW_IQ),
                           2, W_A, 1024)
        kvi_f, kvi_b, kvi_s = _proj_kvi(xn, xs_n, wt, l, ROW_KV, ROW_IK, 1024)
        rest, rest_s = _proj(xn, xs_n, wt, l, lambda j: ROW_REST + j * tn_rest,
                             n_rest // tn_rest, tn_rest, 1024)

        ya = _dsa_prompt(qiq, kvi_b, kvi_f, rest, batch, seq)
        yb, yc, zt = _bc_prompt(rest, conv_w[l], sgu_ln_g[l], sgu_ln_b[l], sgu_w[l], sgu_b[l],
                                batch, seq, min(512, seq))
        outs["kp"].append(kvi_f[:, 0:KV].reshape(batch, seq // PAGE_SIZE, PAGE_SIZE,
                                                 N_KV_HEADS, HEAD_DIM))
        outs["vp"].append(kvi_f[:, KV:2 * KV].reshape(batch, seq // PAGE_SIZE, PAGE_SIZE,
                                                      N_KV_HEADS, HEAD_DIM))
        outs["ikp"].append(kvi_f[:, KVI_IK:KVI_IK + IDX_DIM].reshape(
            batch, seq // PAGE_SIZE, PAGE_SIZE, IDX_DIM))
        outs["cp"].append(zt.reshape(batch, 8, W_B)[:, 8 - (CONV_W - 1):])

        q3 = qiq_s[:, :W_A].reshape(nb, N_HEADS, HEAD_DIM)
        iq3 = qiq_s[:, W_A:].reshape(nb, IDX_HEADS, IDX_DIM)
        kn = kvi_s[:, 0:KV]
        vn = kvi_s[:, KV:2 * KV]
        ikn = kvi_s[:, KVI_IK:KVI_IK + IDX_DIM]
        w3 = kvi_s[:, KVI_IW:KVI_IW + IDX_HEADS].reshape(nb, IDX_HEADS, 1)
        scores = _sample_scores(l, page_table, cache_ik_t, iq3, w3)
        sel, selself = _sample_select(scores, iq3, ikn.reshape(nb, 1, IDX_DIM), w3, topk_s)
        att = _sample_attn(l, page_table, cache_k4, cache_v4, sel, selself, q3,
                           kn.reshape(nb, N_KV_HEADS, HEAD_DIM),
                           vn.reshape(nb, N_KV_HEADS, HEAD_DIM))
        gdim = W_C // C_GROUPS
        w0 = jnp.repeat(sgu_w[l][:, 0, 0], gdim).reshape(1, W_C)
        b0 = jnp.repeat(sgu_b[l][:, 0], gdim).reshape(1, W_C)
        ya_s, yb_s, yc_s, z_s, vn_s = _sample_mid(
            att.reshape(nb, W_A), rest_s, state_conv[l][:, 0], state_conv[l][:, 1],
            conv_w[l], sgu_ln_g[l], sgu_ln_b[l], w0, b0)

        mm, mm_s = _merge((ya, yb, yc), (ya_s, yb_s, yc_s), w_branch, l, rest, rest_s,
                          d, 1024, min(512, d))
        xp, xs = _outproj(mm, mm_s, w_out, l, xp, xs, g_post[l], 512)
        outs["ks"].append(kn.reshape(nb, 1, N_KV_HEADS, HEAD_DIM))
        outs["vs"].append(vn.reshape(nb, 1, N_KV_HEADS, HEAD_DIM))
        outs["iks"].append(ikn.reshape(nb, 1, IDX_DIM))
        outs["cs"].append(jnp.stack([state_conv[l][:, 1], z_s], axis=1))
        outs["chs"].append(vn_s.reshape(nb, 1, W_C))

    st = lambda name: jnp.stack(outs[name])
    return (xp.reshape(batch, seq, d), xs.reshape(nb, 1, d),
            st("kp"), st("vp"), st("ikp"), st("ks"), st("vs"), st("iks"),
            st("cp"), st("cs"), st("chs"))
```

```python
import functools

import jax
import jax.numpy as jnp
from jax import lax
from jax.experimental import pallas as pl
from jax.experimental.pallas import tpu as pltpu

N_HEADS = 8
HEAD_DIM = 128
N_KV_HEADS = 2
GROUP = N_HEADS // N_KV_HEADS
W_A = N_HEADS * HEAD_DIM
KV = N_KV_HEADS * HEAD_DIM
IDX_HEADS = 16
IDX_DIM = 64
TOPK_MAX = 256
Q_BLOCK = 256
PAGE_SIZE = 128
W_B = 1024
CONV_W = 3
W_C = 1024
CHUNK = 128
C_GROUPS = 8
EPS = 1e-6
NEG = -1e30
BIG = 3e38

LANE = 128
KEY_CHUNK = 256
PAGES_PER_STEP = 16
INT_MIN = -(2 ** 31)
INT_MAX = 2 ** 31 - 1
LOG2_E = 1.4426950408889634
BF16_SUBLANES = 16
VT_ROWS = HEAD_DIM + BF16_SUBLANES
COUNT_ROWS = 64
WT_CHUNK = 256

ROW_Q = 0
ROW_KV = ROW_Q + W_A
ROW_IQ = ROW_KV + 2 * KV
ROW_IK = ROW_IQ + IDX_HEADS * IDX_DIM
ROW_REST = ROW_IK + IDX_DIM + IDX_HEADS

KVI_W = KV + KV + 3 * LANE
KVI_IK = 2 * KV
KVI_IW = 2 * KV + 2 * LANE

F32 = jnp.float32
BF16 = jnp.bfloat16


def _silu(x):
    return x / (1.0 + jnp.exp(-x))


def _sigmoid(x):
    return 1.0 / (1.0 + jnp.exp(-x))


def _dot_nt(a, b):
    return lax.dot_general(a, b, (((1,), (1,)), ((), ())), preferred_element_type=F32)


def _ordered_key(x):
    bits = pltpu.bitcast(x, jnp.int32)
    return jnp.where(bits < 0, bits ^ jnp.int32(0x7FFFFFFF), bits)


def _key_to_float(key):
    bits = jnp.where(key < 0, key ^ jnp.int32(0x7FFFFFFF), key)
    return pltpu.bitcast(bits, F32)


def _kth_largest_key(count_ge, shape, k, n_total, count_hi=None):
    if count_hi is None:
        count_coarse, n_coarse = count_ge, 0
    else:
        count_coarse, n_coarse = (lambda c: count_hi((c >> 16).astype(jnp.int16))), 15
    zero = jnp.zeros(shape, jnp.int32)
    c0 = count_coarse(zero)
    t = jnp.where(c0 >= k, zero, jnp.full(shape, INT_MIN, jnp.int32))
    cnt_t = jnp.where(c0 >= k, c0, n_total)

    def step(count):
        def body(it, carry):
            t, cnt_t = carry
            cand = t | jnp.left_shift(jnp.int32(1), jnp.int32(30) - it)
            cnt = count(cand)
            keep = cnt >= k
            return jnp.where(keep, cand, t), jnp.where(keep, cnt, cnt_t)
        return body

    carry = lax.fori_loop(0, n_coarse, step(count_coarse), (t, cnt_t))
    return lax.fori_loop(n_coarse, 31, step(count_ge), carry)


def _rmsnorm_kernel(x_ref, g_ref, o_ref):
    x = x_ref[...]
    ms = jnp.mean(x * x, axis=-1, keepdims=True)
    o_ref[...] = (x * lax.rsqrt(ms + EPS) * g_ref[...]).astype(o_ref.dtype)


def _rmsnorm(x, g, out_dtype, tm):
    m, d = x.shape
    tm = min(tm, m)
    return pl.pallas_call(
        _rmsnorm_kernel,
        grid=(m // tm,),
        in_specs=[pl.BlockSpec((tm, d), lambda i: (i, 0)), pl.BlockSpec((1, d), lambda i: (0, 0))],
        out_specs=pl.BlockSpec((tm, d), lambda i: (i, 0)),
        out_shape=jax.ShapeDtypeStruct((m, d), out_dtype),
        compiler_params=pltpu.CompilerParams(dimension_semantics=("parallel",)),
        name="rmsnorm",
    )(x, g.reshape(1, d))


def _load_transposed(wt_ref, wb_ref, col0=0):
    n, d = wt_ref.shape
    for kb in range(d // WT_CHUNK):
        ks = slice(kb * WT_CHUNK, (kb + 1) * WT_CHUNK)
        wb_ref[ks, col0:col0 + n] = wt_ref[:, ks].T.astype(BF16)


def _proj_kernel(xn_ref, xs_ref, wt_ref, o_ref, os_ref, wb_ref):
    @pl.when(pl.program_id(1) == 0)
    def _():
        _load_transposed(wt_ref, wb_ref)
        os_ref[...] = jnp.dot(xs_ref[...].astype(BF16), wb_ref[...], preferred_element_type=F32)

    o_ref[...] = jnp.dot(xn_ref[...], wb_ref[...],
                         preferred_element_type=F32).astype(o_ref.dtype)


def _proj(xn, xs_n, wt, layer, row_of_tile, n_tiles, tn, tm):
    m, d = xn.shape
    nbs = xs_n.shape[0]
    tm = min(tm, m)
    return pl.pallas_call(
        _proj_kernel,
        grid=(n_tiles, m // tm),
        in_specs=[
            pl.BlockSpec((tm, d), lambda j, i: (i, 0)),
            pl.BlockSpec((nbs, d), lambda j, i: (0, 0)),
            pl.BlockSpec((None, pl.Element(tn), pl.Element(d)),
                         lambda j, i: (layer, pl.multiple_of(row_of_tile(j), 8), 0)),
        ],
        out_specs=[pl.BlockSpec((tm, tn), lambda j, i: (i, j)),
                   pl.BlockSpec((nbs, tn), lambda j, i: (0, j))],
        out_shape=[jax.ShapeDtypeStruct((m, n_tiles * tn), BF16),
                   jax.ShapeDtypeStruct((nbs, n_tiles * tn), F32)],
        scratch_shapes=[pltpu.VMEM((d, tn), BF16)],
        compiler_params=pltpu.CompilerParams(
            dimension_semantics=("arbitrary", "arbitrary"),
            vmem_limit_bytes=56 * 1024 * 1024),
        name="proj",
    )(xn, xs_n, wt)


def _proj_kvi_kernel(xn_ref, xs_ref, wkv_ref, wix_ref, of_ref, ob_ref, os_ref, wb_ref):
    @pl.when(pl.program_id(0) == 0)
    def _():
        _load_transposed(wkv_ref, wb_ref)
        d = wb_ref.shape[0]
        lane = lax.broadcasted_iota(jnp.int32, (WT_CHUNK, LANE), 1)
        for kb in range(d // WT_CHUNK):
            ks = slice(kb * WT_CHUNK, (kb + 1) * WT_CHUNK)
            t = wix_ref[:, ks].T
            r = pltpu.roll(t, LANE // 2, 1)
            wb_ref[ks, KVI_IK:KVI_IK + LANE] = jnp.where(lane < IDX_DIM, t, 0.0).astype(BF16)
            wb_ref[ks, KVI_IK + LANE:KVI_IW] = jnp.where(lane >= IDX_DIM, r, 0.0).astype(BF16)
            wb_ref[ks, KVI_IW:KVI_W] = jnp.where(lane < IDX_HEADS, r, 0.0).astype(BF16)
        os_ref[...] = jnp.dot(xs_ref[...].astype(BF16), wb_ref[...], preferred_element_type=F32)

    acc = jnp.dot(xn_ref[...], wb_ref[...], preferred_element_type=F32)
    of_ref[...] = acc
    ob_ref[...] = acc.astype(BF16)


def _proj_kvi(xn, xs_n, wt, layer, kv_row, ik_row, tm):
    m, d = xn.shape
    nbs = xs_n.shape[0]
    tm = min(tm, m)
    return pl.pallas_call(
        _proj_kvi_kernel,
        grid=(m // tm,),
        in_specs=[
            pl.BlockSpec((tm, d), lambda i: (i, 0)),
            pl.BlockSpec((nbs, d), lambda i: (0, 0)),
            pl.BlockSpec((None, pl.Element(2 * KV), pl.Element(d)), lambda i: (layer, kv_row, 0)),
            pl.BlockSpec((None, pl.Element(LANE), pl.Element(d)), lambda i: (layer, ik_row, 0)),
        ],
        out_specs=[pl.BlockSpec((tm, KVI_W), lambda i: (i, 0)),
                   pl.BlockSpec((tm, KVI_W), lambda i: (i, 0)),
                   pl.BlockSpec((nbs, KVI_W), lambda i: (0, 0))],
        out_shape=[jax.ShapeDtypeStruct((m, KVI_W), F32),
                   jax.ShapeDtypeStruct((m, KVI_W), BF16),
                   jax.ShapeDtypeStruct((nbs, KVI_W), F32)],
        scratch_shapes=[pltpu.VMEM((d, KVI_W), BF16)],
        compiler_params=pltpu.CompilerParams(
            dimension_semantics=("arbitrary",),
            vmem_limit_bytes=56 * 1024 * 1024),
        name="proj_kvi",
    )(xn, xs_n, wt, wt)


def _transpose_bf16(x):
    return x.astype(F32).T.astype(BF16)


def _dsa_prompt_kernel(q_ref, iq_ref, w_ref, k_ref, v_ref, ik_ref, ga_ref, o_ref,
                       st_ref, key_ref, khi_ref, iqt_ref, qgt_ref, wt_ref, vt_ref,
                       m0_ref, m1_ref, acc0_ref, acc1_ref, sba_ref, sbb_ref, *, topk):
    i = pl.program_id(1)
    qb = q_ref.shape[0]
    seq = k_ref.shape[0]
    kc = KEY_CHUNK
    nk = (i + 1) * qb
    nch = (nk + kc - 1) // kc
    qpos = i * qb + lax.broadcasted_iota(jnp.int32, (kc, qb), 1)
    krow = lax.broadcasted_iota(jnp.int32, (kc, qb), 0)

    @pl.when(i == 0)
    def _():
        def body(c, carry):
            off = pl.multiple_of(c * kc, kc)
            vt = _transpose_bf16(v_ref[pl.ds(off, kc), :])
            for g in range(N_KV_HEADS):
                vt_ref[g, 0:HEAD_DIM, pl.ds(off, kc)] = vt[g * HEAD_DIM:(g + 1) * HEAD_DIM]
                vt_ref[g, HEAD_DIM:VT_ROWS, pl.ds(off, kc)] = jnp.ones(
                    (VT_ROWS - HEAD_DIM, kc), BF16)
            return carry
        lax.fori_loop(0, seq // kc, body, 0)

    n_tiles = IDX_HEADS // 2
    for j in range(n_tiles):
        iqt_ref[:, j * qb:(j + 1) * qb] = _transpose_bf16(iq_ref[:, j * LANE:(j + 1) * LANE])
    for g in range(N_KV_HEADS):
        for hl in range(GROUP):
            h = g * GROUP + hl
            qgt_ref[g, :, hl * qb:(hl + 1) * qb] = _transpose_bf16(
                q_ref[:, h * HEAD_DIM:(h + 1) * HEAD_DIM])
    wt_ref[...] = (w_ref[...] * ((IDX_HEADS ** -0.5) * (IDX_DIM ** -0.5))).T

    def score_chunk(c):
        off = pl.multiple_of(c * kc, kc)
        acc = jnp.zeros((kc, qb), F32)
        for half in range(2):
            res = jnp.dot(ik_ref[pl.ds(off, kc), half * LANE:(half + 1) * LANE], iqt_ref[...],
                          preferred_element_type=F32)
            for j in range(n_tiles):
                h = 2 * j + half
                acc = acc + jnp.maximum(res[:, j * qb:(j + 1) * qb], 0.0) * wt_ref[h:h + 1, :]
        sc = jnp.where(off + krow <= qpos, acc, NEG)
        key = _ordered_key(sc)
        key_ref[pl.ds(off, kc), :] = key
        khi_ref[pl.ds(off, kc), :] = (key >> 16).astype(jnp.int16)

    def score_body(c2, carry):
        score_chunk(2 * c2)
        score_chunk(2 * c2 + 1)
        return carry

    lax.fori_loop(0, (nch + 1) // 2, score_body, 0)

    def count_ge(cand):
        def body(c, acc):
            off = pl.multiple_of(c * kc, kc)
            ge = jnp.where(key_ref[pl.ds(off, kc), :] >= cand, 1.0, 0.0)
            return acc + jnp.sum(ge.reshape(kc // COUNT_ROWS, COUNT_ROWS, qb), axis=0)
        acc = lax.fori_loop(0, nch, body, jnp.zeros((COUNT_ROWS, qb), F32))
        return jnp.sum(acc, axis=0, keepdims=True)

    def count_hi(cand16):
        rows = 2 * COUNT_ROWS
        def body(c, acc):
            off = pl.multiple_of(c * kc, kc)
            ge = jnp.where(khi_ref[pl.ds(off, kc), :] >= cand16, jnp.int16(1), jnp.int16(0))
            for r in range(kc // rows):
                acc = acc + ge[r * rows:(r + 1) * rows]
            return acc
        acc = lax.fori_loop(0, nch, body, jnp.zeros((rows, qb), jnp.int16))
        return jnp.sum(acc.astype(F32), axis=0, keepdims=True)

    t_key, cnt_ge = _kth_largest_key(count_ge, (1, qb), float(topk), (nch * kc).astype(F32),
                                     count_hi)

    at_top = t_key == INT_MAX
    cnt_gt = jnp.where(at_top, 0.0, count_ge(jnp.where(at_top, t_key, t_key + 1)))
    need = float(topk) - cnt_gt
    excess_ties = jnp.max(cnt_ge - cnt_gt - need) > 0.0

    def write_bounds(selected):
        def body(c, carry):
            off = pl.multiple_of(c * kc, kc)
            kidx = off + krow
            bound = jnp.where(selected(key_ref[pl.ds(off, kc), :], kidx), BIG, NEG)
            st_ref[pl.ds(off, kc), :] = jnp.where(kidx <= qpos, bound, NEG)
            return carry
        lax.fori_loop(0, nch, body, 0)

    @pl.when(jnp.logical_not(excess_ties))
    def _():
        write_bounds(lambda kk, kidx: kk >= t_key)

    @pl.when(excess_ties)
    def _():
        def tied_before(p):
            def body(c, acc):
                off = pl.multiple_of(c * kc, kc)
                hit = (key_ref[pl.ds(off, kc), :] == t_key) & (off + krow < p)
                hit = jnp.where(hit, 1.0, 0.0)
                return acc + jnp.sum(hit.reshape(kc // COUNT_ROWS, COUNT_ROWS, qb), axis=0)
            acc = lax.fori_loop(0, nch, body, jnp.zeros((COUNT_ROWS, qb), F32))
            return jnp.sum(acc, axis=0, keepdims=True)

        n_bits = max(1, (seq - 1).bit_length())

        def body(it, p):
            cand = p | jnp.left_shift(jnp.int32(1), jnp.int32(n_bits - 1) - it)
            return jnp.where(tied_before(cand) < need, cand, p)

        cut = lax.fori_loop(0, n_bits, body, jnp.zeros((1, qb), jnp.int32))
        write_bounds(lambda kk, kidx: (kk > t_key) | ((kk == t_key) & (kidx <= cut)))

    exp2_scale = (HEAD_DIM ** -0.5) * LOG2_E
    m_refs = (m0_ref, m1_ref)
    acc_refs = (acc0_ref, acc1_ref)
    for g in range(N_KV_HEADS):
        m_refs[g][...] = jnp.full(m_refs[g].shape, -jnp.inf, F32)
        acc_refs[g][...] = jnp.zeros(acc_refs[g].shape, F32)

    def logits(c, dst_ref):
        off = pl.multiple_of(jnp.minimum(c, nch - 1) * kc, kc)
        for g in range(N_KV_HEADS):
            dst_ref[g] = jnp.dot(
                k_ref[pl.ds(off, kc), g * HEAD_DIM:(g + 1) * HEAD_DIM], qgt_ref[g],
                preferred_element_type=F32)

    def softmax_pv(c, src_ref):
        off = pl.multiple_of(jnp.minimum(c, nch - 1) * kc, kc)
        bound = jnp.where(c < nch, st_ref[pl.ds(off, kc), :], NEG)
        bound = jnp.concatenate([bound] * GROUP, axis=1)
        for g in range(N_KV_HEADS):
            s = jnp.minimum(src_ref[g], bound)
            m_prev = m_refs[g][...]
            m_new = jnp.maximum(m_prev, jnp.max(s, axis=0, keepdims=True))
            alpha = jnp.exp2((m_prev - m_new) * exp2_scale)
            p = jnp.exp2((s - m_new) * exp2_scale)
            acc_refs[g][...] = alpha * acc_refs[g][...] + jnp.dot(
                vt_ref[g, :, pl.ds(off, kc)], p.astype(BF16), preferred_element_type=F32)
            m_refs[g][...] = m_new

    logits(0, sba_ref)

    def attn_body(c2, carry):
        c = 2 * c2
        logits(c + 1, sbb_ref)
        softmax_pv(c, sba_ref)
        logits(c + 2, sba_ref)
        softmax_pv(c + 1, sbb_ref)
        return carry

    lax.fori_loop(0, (nch + 1) // 2, attn_body, 0)
    for g in range(N_KV_HEADS):
        acc = acc_refs[g][...]
        o_t = acc[0:HEAD_DIM] / acc[HEAD_DIM:HEAD_DIM + 1]
        for hl in range(GROUP):
            h = g * GROUP + hl
            cols = slice(h * HEAD_DIM, (h + 1) * HEAD_DIM)
            gate = _silu(ga_ref[:, cols].astype(F32))
            o_ref[:, cols] = (o_t[:, hl * qb:(hl + 1) * qb].T * gate).astype(o_ref.dtype)


def _dsa_prompt(qiq, kvi_b, kvi_f, rest, batch, seq):
    m = qiq.shape[0]
    qb = Q_BLOCK
    nq = seq // qb
    topk = min(TOPK_MAX, seq // 4)
    assert seq % (2 * KEY_CHUNK) == 0 and topk <= KEY_CHUNK
    row = lambda b, i: b * nq + i
    return pl.pallas_call(
        functools.partial(_dsa_prompt_kernel, topk=topk),
        grid=(batch, nq),
        in_specs=[
            pl.BlockSpec((qb, W_A), lambda b, i: (row(b, i), 0)),
            pl.BlockSpec((qb, IDX_HEADS * IDX_DIM), lambda b, i: (row(b, i), 1)),
            pl.BlockSpec((qb, LANE), lambda b, i: (row(b, i), KVI_IW // LANE)),
            pl.BlockSpec((seq, KV), lambda b, i: (b, 0)),
            pl.BlockSpec((seq, KV), lambda b, i: (b, 1)),
            pl.BlockSpec((seq, 2 * LANE), lambda b, i: (b, KVI_IK // (2 * LANE))),
            pl.BlockSpec((qb, W_A), lambda b, i: (row(b, i), 0)),
        ],
        out_specs=pl.BlockSpec((qb, W_A), lambda b, i: (row(b, i), 0)),
        out_shape=jax.ShapeDtypeStruct((m, W_A), BF16),
        scratch_shapes=[
            pltpu.VMEM((seq, qb), F32),
            pltpu.VMEM((seq, qb), jnp.int32),
            pltpu.VMEM((seq, qb), jnp.int16),
            pltpu.VMEM((LANE, IDX_HEADS // 2 * qb), BF16),
            pltpu.VMEM((N_KV_HEADS, HEAD_DIM, GROUP * qb), BF16),
            pltpu.VMEM((LANE, qb), F32),
            pltpu.VMEM((N_KV_HEADS, VT_ROWS, seq), BF16),
            pltpu.VMEM((1, GROUP * qb), F32),
            pltpu.VMEM((1, GROUP * qb), F32),
            pltpu.VMEM((VT_ROWS, GROUP * qb), F32),
            pltpu.VMEM((VT_ROWS, GROUP * qb), F32),
            pltpu.VMEM((N_KV_HEADS, KEY_CHUNK, GROUP * qb), F32),
            pltpu.VMEM((N_KV_HEADS, KEY_CHUNK, GROUP * qb), F32),
        ],
        compiler_params=pltpu.CompilerParams(
            dimension_semantics=("parallel", "arbitrary"),
            vmem_limit_bytes=56 * 1024 * 1024),
        name="dsa_prompt",
    )(qiq, qiq, kvi_f, kvi_b, kvi_b, kvi_b, rest)


PREV_ROWS = 16


def _bc_prompt_kernel(h_ref, bg_ref, cg_ref, gb_ref, u_ref, vc_ref, gc_ref, ph_ref, pcg_ref,
                      cw_ref, lg_ref, lb_ref, ws_ref, bst_ref,
                      yb_ref, yc_ref, zt_ref, zs_ref):
    tm = h_ref.shape[0]
    j = pl.program_id(1)
    z = cg_ref[...].astype(F32) * h_ref[...].astype(F32)
    pz = pcg_ref[...].astype(F32) * ph_ref[...].astype(F32)
    pz = jnp.where(j == 0, 0.0, pz)
    zs_ref[0:PREV_ROWS, :] = pz
    zs_ref[PREV_ROWS:PREV_ROWS + tm, :] = z
    z1 = zs_ref[PREV_ROWS - 1:PREV_ROWS - 1 + tm, :]
    z2 = zs_ref[PREV_ROWS - 2:PREV_ROWS - 2 + tm, :]
    conv = cw_ref[0:1, :] * z2 + cw_ref[1:2, :] * z1 + cw_ref[2:3, :] * z
    yb = bg_ref[...].astype(F32) * conv * _silu(gb_ref[...].astype(F32))
    yb_ref[...] = yb.astype(yb_ref.dtype)
    zt_ref[...] = z[tm - 8:tm]

    vc = vc_ref[...].astype(F32)
    mu = jnp.mean(vc, axis=-1, keepdims=True)
    dv = vc - mu
    var = jnp.mean(dv * dv, axis=-1, keepdims=True)
    vn = (dv * lax.rsqrt(var + EPS) * lg_ref[...] + lb_ref[...]).astype(BF16)
    rr = lax.broadcasted_iota(jnp.int32, (CHUNK, CHUNK), 0)
    cc = lax.broadcasted_iota(jnp.int32, (CHUNK, CHUNK), 1)
    gdim = W_C // C_GROUPS
    for g in range(C_GROUPS):
        wg = jnp.where(rr >= cc, ws_ref[g], 0.0).astype(BF16)
        bias = bst_ref[:, g:g + 1]
        cols = slice(g * gdim, (g + 1) * gdim)
        for c in range(tm // CHUNK):
            rows = slice(c * CHUNK, (c + 1) * CHUNK)
            mixed = jnp.dot(wg, vn[rows, cols], preferred_element_type=F32) + bias
            yc = u_ref[rows, cols].astype(F32) * mixed * _silu(gc_ref[rows, cols].astype(F32))
            yc_ref[rows, cols] = yc.astype(yc_ref.dtype)


def _bc_prompt(rest, conv_w, ln_g, ln_b, ws, bs, batch, seq, tm):
    m = rest.shape[0]
    nt = seq // tm
    row = lambda b, j: b * nt + j
    prev = lambda b, j: jnp.maximum((b * seq + j * tm) // PREV_ROWS - 1, 0)
    col = lambda c: (lambda b, j: (row(b, j), c))
    pcol = lambda c: (lambda b, j: (prev(b, j), c))
    const2 = lambda b, j: (0, 0)
    yb, yc, zt = pl.pallas_call(
        _bc_prompt_kernel,
        grid=(batch, nt),
        in_specs=[pl.BlockSpec((tm, W_B), col(c)) for c in range(1, 8)] + [
            pl.BlockSpec((PREV_ROWS, W_B), pcol(1)),
            pl.BlockSpec((PREV_ROWS, W_B), pcol(3)),
            pl.BlockSpec((CONV_W, W_B), const2),
            pl.BlockSpec((1, W_C), const2),
            pl.BlockSpec((1, W_C), const2),
            pl.BlockSpec((C_GROUPS, CHUNK, CHUNK), lambda b, j: (0, 0, 0)),
            pl.BlockSpec((CHUNK, C_GROUPS), const2),
        ],
        out_specs=[
            pl.BlockSpec((tm, W_B), lambda b, j: (row(b, j), 0)),
            pl.BlockSpec((tm, W_C), lambda b, j: (row(b, j), 0)),
            pl.BlockSpec((8, W_B), lambda b, j: (b, 0)),
        ],
        out_shape=[
            jax.ShapeDtypeStruct((m, W_B), BF16),
            jax.ShapeDtypeStruct((m, W_C), BF16),
            jax.ShapeDtypeStruct((batch * 8, W_B), F32),
        ],
        scratch_shapes=[pltpu.VMEM((tm + PREV_ROWS, W_B), F32)],
        compiler_params=pltpu.CompilerParams(
            dimension_semantics=("parallel", "arbitrary"),
            vmem_limit_bytes=56 * 1024 * 1024),
        name="bc_prompt",
    )(*([rest] * 7), rest, rest, conv_w, ln_g.reshape(1, W_C), ln_b.reshape(1, W_C), ws, bs.T)
    return yb, yc, zt


N_BRANCH = 3


def _merge_kernel(*refs):
    y_refs, ys_refs = refs[0:3], refs[3:6]
    w_refs = refs[6:9]
    r_refs, rs_refs = refs[9:12], refs[12:15]
    o_ref, os_ref, wbf_ref = refs[15:18]

    def merged(ys, rs):
        out = None
        for b in range(N_BRANCH):
            proj = jnp.dot(ys[b][...].astype(BF16), wbf_ref[b], preferred_element_type=F32)
            term = _sigmoid(rs[b][...].astype(F32)) * proj
            out = term if out is None else out + term
        return out

    @pl.when(pl.program_id(1) == 0)
    def _():
        for b in range(N_BRANCH):
            wbf_ref[b] = w_refs[b][...].astype(BF16)
        os_ref[...] = merged(ys_refs, rs_refs)

    o_ref[...] = merged(y_refs, r_refs).astype(o_ref.dtype)


def _merge(ys, ys_s, w_branch, layer, rest, rest_s, d, tm, tn):
    m = ys[0].shape[0]
    nbs = ys_s[0].shape[0]
    tm = min(tm, m)
    nb = d // tn
    r0 = 8 * W_A // tn
    y_spec = pl.BlockSpec((tm, W_A), lambda j, i: (i, 0))
    ys_spec = pl.BlockSpec((nbs, W_A), lambda j, i: (0, 0))
    w_spec = lambda b: pl.BlockSpec((None, W_A, tn), lambda j, i: (layer, b, j))
    r_spec = lambda b: pl.BlockSpec((tm, tn), lambda j, i: (i, r0 + b * nb + j))
    rs_spec = lambda b: pl.BlockSpec((nbs, tn), lambda j, i: (0, r0 + b * nb + j))
    rng = range(N_BRANCH)
    return pl.pallas_call(
        _merge_kernel,
        grid=(nb, m // tm),
        in_specs=([y_spec] * 3 + [ys_spec] * 3 + [w_spec(b) for b in rng]
                  + [r_spec(b) for b in rng] + [rs_spec(b) for b in rng]),
        out_specs=[pl.BlockSpec((tm, tn), lambda j, i: (i, j)),
                   pl.BlockSpec((nbs, tn), lambda j, i: (0, j))],
        out_shape=[jax.ShapeDtypeStruct((m, d), BF16), jax.ShapeDtypeStruct((nbs, d), F32)],
        scratch_shapes=[pltpu.VMEM((N_BRANCH, W_A, tn), BF16)],
        compiler_params=pltpu.CompilerParams(
            dimension_semantics=("arbitrary", "arbitrary"),
            vmem_limit_bytes=56 * 1024 * 1024),
        name="merge",
    )(*ys, *ys_s, *([w_branch] * 3), *([rest] * 3), *([rest_s] * 3))


def _outproj_kernel(m_ref, ms_ref, w_ref, x_ref, xs_ref, g_ref, o_ref, os_ref, wbf_ref):
    def residual(m, x):
        out = jnp.dot(m.astype(BF16), wbf_ref[...], preferred_element_type=F32)
        ms = jnp.mean(out * out, axis=-1, keepdims=True)
        return x + out * lax.rsqrt(ms + EPS) * g_ref[...]

    @pl.when(pl.program_id(0) == 0)
    def _():
        wbf_ref[...] = w_ref[...].astype(BF16)
        os_ref[...] = residual(ms_ref[...], xs_ref[...])

    o_ref[...] = residual(m_ref[...], x_ref[...])


def _outproj(mm, mm_s, w_out, layer, x, xs, g, tm):
    m, d = x.shape
    nbs = xs.shape[0]
    tm = min(tm, m)
    row_spec = pl.BlockSpec((tm, d), lambda i: (i, 0))
    s_spec = pl.BlockSpec((nbs, d), lambda i: (0, 0))
    return pl.pallas_call(
        _outproj_kernel,
        grid=(m // tm,),
        in_specs=[
            row_spec, s_spec,
            pl.BlockSpec((None, d, d), lambda i: (layer, 0, 0), pipeline_mode=pl.Buffered(1)),
            row_spec, s_spec,
            pl.BlockSpec((1, d), lambda i: (0, 0)),
        ],
        out_specs=[row_spec, s_spec],
        out_shape=[jax.ShapeDtypeStruct((m, d), F32), jax.ShapeDtypeStruct((nbs, d), F32)],
        scratch_shapes=[pltpu.VMEM((d, d), BF16)],
        compiler_params=pltpu.CompilerParams(
            dimension_semantics=("arbitrary",),
            vmem_limit_bytes=56 * 1024 * 1024),
        name="outproj",
    )(mm, mm_s, w_out, x, xs, g.reshape(1, d))


def _sample_scores_kernel(pt_ref, *refs):
    del pt_ref
    npg = PAGES_PER_STEP
    page_refs = refs[:npg]
    iq_ref, w_ref, o_ref = refs[npg:]
    iq = iq_ref[...].astype(BF16)
    w = w_ref[...] * ((IDX_HEADS ** -0.5) * (IDX_DIM ** -0.5))
    for u in range(npg):
        s = jnp.dot(iq, page_refs[u][...].astype(BF16),
                    preferred_element_type=F32)
        o_ref[u:u + 1, :] = jnp.sum(jnp.maximum(s, 0.0) * w, axis=0, keepdims=True)


def _sample_scores(layer, page_table, cache_ik_t, iq3, w3):
    nb, n_pages = page_table.shape
    npg = PAGES_PER_STEP
    assert n_pages % npg == 0
    page_spec = lambda u: pl.BlockSpec(
        (None, None, IDX_DIM, PAGE_SIZE),
        lambda s, j, pt: (layer, pt[s, j * npg + u], 0, 0))
    grid_spec = pltpu.PrefetchScalarGridSpec(
        num_scalar_prefetch=1,
        grid=(nb, n_pages // npg),
        in_specs=[page_spec(u) for u in range(npg)] + [
            pl.BlockSpec((None, IDX_HEADS, IDX_DIM), lambda s, j, pt: (s, 0, 0)),
            pl.BlockSpec((None, IDX_HEADS, 1), lambda s, j, pt: (s, 0, 0)),
        ],
        out_specs=pl.BlockSpec((None, npg, PAGE_SIZE), lambda s, j, pt: (s, j, 0)),
    )
    return pl.pallas_call(
        _sample_scores_kernel,
        grid_spec=grid_spec,
        out_shape=jax.ShapeDtypeStruct((nb, n_pages, PAGE_SIZE), F32),
        compiler_params=pltpu.CompilerParams(dimension_semantics=("parallel", "arbitrary")),
        name="sample_scores",
    )(page_table, *([cache_ik_t] * npg), iq3, w3)


def _sample_select_kernel(sc_ref, iq_ref, ikn_ref, w_ref, sel_ref, selself_ref, *, topk):
    nb = sc_ref.shape[0]
    w = w_ref[...] * ((IDX_HEADS ** -0.5) * (IDX_DIM ** -0.5))
    s_self = jnp.sum(iq_ref[...] * ikn_ref[...], axis=-1, keepdims=True)
    sc_self = jnp.sum(jnp.maximum(s_self, 0.0) * w, axis=1, keepdims=True)
    sc = sc_ref[...]
    key = _ordered_key(sc)
    key_self = _ordered_key(sc_self)

    def count_ge(cand):
        ge = jnp.where(key >= cand, 1.0, 0.0)
        cnt = jnp.sum(jnp.sum(ge, axis=2, keepdims=True), axis=1, keepdims=True)
        return cnt + jnp.where(key_self >= cand, 1.0, 0.0)

    n_past = sc.shape[1] * sc.shape[2]
    t_key, _ = _kth_largest_key(count_ge, (nb, 1, 1), float(topk), float(n_past + 1))

    at_top = t_key == INT_MAX
    cnt_gt = jnp.where(at_top, 0.0, count_ge(jnp.where(at_top, t_key, t_key + 1)))
    need = float(topk) - cnt_gt
    pos = (lax.broadcasted_iota(jnp.int32, sc.shape, 1) * sc.shape[2]
           + lax.broadcasted_iota(jnp.int32, sc.shape, 2))
    tied = key == t_key

    def tied_before(p):
        hit = jnp.where(tied & (pos < p), 1.0, 0.0)
        return jnp.sum(jnp.sum(hit, axis=2, keepdims=True), axis=1, keepdims=True)

    n_bits = n_past.bit_length()

    def body(it, p):
        cand = p | jnp.left_shift(jnp.int32(1), jnp.int32(n_bits - 1) - it)
        return jnp.where(tied_before(cand) < need, cand, p)

    cut = lax.fori_loop(0, n_bits, body, jnp.zeros((nb, 1, 1), jnp.int32))
    sel_ref[...] = jnp.where((key > t_key) | (tied & (pos <= cut)), 1.0, 0.0)
    self_sel = (key_self > t_key) | ((key_self == t_key) & (n_past <= cut))
    selself_ref[...] = jnp.broadcast_to(jnp.where(self_sel, 1.0, 0.0), selself_ref.shape)


def _sample_select(scores, iq3, ikn3, w3, topk):
    nb, n_pages, _ = scores.shape
    return pl.pallas_call(
        functools.partial(_sample_select_kernel, topk=topk),
        out_shape=[jax.ShapeDtypeStruct((nb, n_pages, PAGE_SIZE), F32),
                   jax.ShapeDtypeStruct((nb, 1, LANE), F32)],
        name="sample_select",
    )(scores, iq3, ikn3, w3)


def _sample_attn_kernel(pt_ref, *refs):
    del pt_ref
    npg = PAGES_PER_STEP
    k_refs = refs[:npg]
    v_refs = refs[npg:2 * npg]
    sel_ref, selself_ref, q_ref, kn_ref, vn_ref, o_ref, m_ref, l_ref, acc_ref = refs[2 * npg:]
    j = pl.program_id(1)
    scale = HEAD_DIM ** -0.5
    rows2 = N_KV_HEADS * PAGE_SIZE

    @pl.when(j == 0)
    def _():
        m_ref[...] = jnp.full(m_ref.shape, -jnp.inf, F32)
        l_ref[...] = jnp.zeros(l_ref.shape, F32)
        acc_ref[...] = jnp.zeros(acc_ref.shape, F32)

    q = q_ref[...].astype(BF16)
    er = lax.broadcasted_iota(jnp.int32, (PAGE_SIZE, rows2), 0)
    ec = lax.broadcasted_iota(jnp.int32, (PAGE_SIZE, rows2), 1)
    expand = jnp.where(ec // N_KV_HEADS == er, 1.0, 0.0).astype(BF16)
    sel2 = jnp.dot(sel_ref[...].astype(BF16), expand, preferred_element_type=F32)
    hh = lax.broadcasted_iota(jnp.int32, (N_HEADS, rows2), 0)
    cc = lax.broadcasted_iota(jnp.int32, (N_HEADS, rows2), 1)
    head_ok = (hh // GROUP) == (cc % N_KV_HEADS)

    def update(s, v):
        m_prev = m_ref[...]
        m_new = jnp.maximum(m_prev, jnp.max(s, axis=-1, keepdims=True))
        alpha = jnp.exp(m_prev - m_new)
        p = jnp.exp(s - m_new)
        l_ref[...] = alpha * l_ref[...] + jnp.sum(p, axis=-1, keepdims=True)
        acc_ref[...] = alpha * acc_ref[...] + v(p)
        m_ref[...] = m_new

    logits = []
    for u in range(npg):
        s = _dot_nt(q, k_refs[u][...].astype(BF16)) * scale
        logits.append(jnp.where(head_ok & (sel2[u:u + 1, :] > 0.5), s, NEG))

    def weighted_values(p):
        pb = p.astype(BF16)
        out = jnp.zeros((N_HEADS, HEAD_DIM), F32)
        for u in range(npg):
            out = out + jnp.dot(pb[:, u * rows2:(u + 1) * rows2], v_refs[u][...].astype(BF16),
                                preferred_element_type=F32)
        return out

    update(jnp.concatenate(logits, axis=1), weighted_values)

    @pl.when(j == pl.num_programs(1) - 1)
    def _():
        hrow = lax.broadcasted_iota(jnp.int32, (N_HEADS, HEAD_DIM), 0)
        kn = kn_ref[...]
        vn = vn_ref[...]
        k_self = jnp.where(hrow < GROUP, kn[0:1, :], kn[1:2, :])
        v_self = jnp.where(hrow < GROUP, vn[0:1, :], vn[1:2, :])
        s_self = jnp.sum(q_ref[...] * k_self, axis=-1, keepdims=True) * scale
        s_self = jnp.where(selself_ref[0:1, 0:1] > 0.5, s_self, NEG)
        update(s_self, lambda p: p * v_self)
        o_ref[...] = acc_ref[...] / l_ref[...]


def _sample_attn(layer, page_table, cache_k4, cache_v4, sel, selself, q3, kn3, vn3):
    nb, n_pages = page_table.shape
    npg = PAGES_PER_STEP
    rows2 = N_KV_HEADS * PAGE_SIZE
    page_spec = lambda u: pl.BlockSpec(
        (None, None, rows2, HEAD_DIM),
        lambda s, j, pt: (layer, pt[s, j * npg + u], 0, 0))
    per_seq = lambda shape: pl.BlockSpec((None,) + shape, lambda s, j, pt: (s, 0, 0))
    grid_spec = pltpu.PrefetchScalarGridSpec(
        num_scalar_prefetch=1,
        grid=(nb, n_pages // npg),
        in_specs=[page_spec(u) for u in range(npg)] * 2 + [
            pl.BlockSpec((None, npg, PAGE_SIZE), lambda s, j, pt: (s, j, 0)),
            per_seq((1, LANE)),
            per_seq((N_HEADS, HEAD_DIM)),
            per_seq((N_KV_HEADS, HEAD_DIM)),
            per_seq((N_KV_HEADS, HEAD_DIM)),
        ],
        out_specs=per_seq((N_HEADS, HEAD_DIM)),
        scratch_shapes=[
            pltpu.VMEM((N_HEADS, 1), F32),
            pltpu.VMEM((N_HEADS, 1), F32),
            pltpu.VMEM((N_HEADS, HEAD_DIM), F32),
        ],
    )
    return pl.pallas_call(
        _sample_attn_kernel,
        grid_spec=grid_spec,
        out_shape=jax.ShapeDtypeStruct((nb, N_HEADS, HEAD_DIM), F32),
        compiler_params=pltpu.CompilerParams(dimension_semantics=("parallel", "arbitrary")),
        name="sample_attn",
    )(page_table, *([cache_k4] * npg), *([cache_v4] * npg), sel, selself, q3, kn3, vn3)


def _sample_mid_kernel(att_ref, rest_ref, s0_ref, s1_ref, cw_ref, lg_ref, lb_ref, w0_ref, b0_ref,
                       ya_ref, yb_ref, yc_ref, z_ref, vn_ref):
    blk = lambda c: rest_ref[:, c * W_A:(c + 1) * W_A]
    ga, h, bg, cg, gb, u, vc, gc = [blk(c) for c in range(8)]
    ya_ref[...] = att_ref[...] * _silu(ga)
    z = cg * h
    conv = cw_ref[0:1, :] * s0_ref[...] + cw_ref[1:2, :] * s1_ref[...] + cw_ref[2:3, :] * z
    yb_ref[...] = bg * conv * _silu(gb)
    z_ref[...] = z
    mu = jnp.mean(vc, axis=-1, keepdims=True)
    dv = vc - mu
    var = jnp.mean(dv * dv, axis=-1, keepdims=True)
    vn = dv * lax.rsqrt(var + EPS) * lg_ref[...] + lb_ref[...]
    vn_ref[...] = vn
    mixed = w0_ref[...] * vn + b0_ref[...]
    yc_ref[...] = u * mixed * _silu(gc)


def _sample_mid(att, rest, s0, s1, conv_w, ln_g, ln_b, w0, b0):
    nb = att.shape[0]
    shp = jax.ShapeDtypeStruct((nb, W_A), F32)
    return pl.pallas_call(
        _sample_mid_kernel,
        out_shape=[shp] * 5,
        name="sample_mid",
    )(att, rest, s0, s1, conv_w, ln_g.reshape(1, W_C), ln_b.reshape(1, W_C), w0, b0)


def kernel(x_prompt, x_sample, cache_k, cache_v, cache_idx_k, state_conv, page_table,
           g_pre, g_post, w_in, conv_w, sgu_ln_g, sgu_ln_b, sgu_w, sgu_b, w_branch, w_out):
    depth = w_in.shape[0]
    batch, seq, d = x_prompt.shape
    nb = x_sample.shape[0]
    n_pool = cache_k.shape[1]
    n_pages = page_table.shape[1]
    topk_s = min(TOPK_MAX, (n_pages * PAGE_SIZE + 1) // 4)
    assert x_sample.shape[1] == 1

    xp = x_prompt.reshape(batch * seq, d)
    xs = x_sample.reshape(nb, d)
    cache_k4 = cache_k.reshape(depth, n_pool, PAGE_SIZE * N_KV_HEADS, HEAD_DIM)
    cache_v4 = cache_v.reshape(depth, n_pool, PAGE_SIZE * N_KV_HEADS, HEAD_DIM)
    cache_ik_t = jnp.swapaxes(cache_idx_k, 2, 3)

    wt = jnp.swapaxes(w_in, 1, 2)
    n_rest = wt.shape[1] - ROW_REST
    tn_rest = 1024
    while n_rest % tn_rest:
        tn_rest //= 2
    assert tn_rest % LANE == 0

    outs = {name: [] for name in ("kp", "vp", "ikp", "ks", "vs", "iks", "cp", "cs", "chs")}
    for l in range(depth):
        xn = _rmsnorm(xp, g_pre[l], BF16, 1024)
        xs_n = _rmsnorm(xs, g_pre[l], F32, 8)
        qiq, qiq_s = _proj(xn, xs_n, wt, l, lambda j: jnp.where(j == 0, ROW_Q, ROW_IQ),
                           2, W_A, 1024)
        kvi_f, kvi_b, kvi_s = _proj_kvi(xn, xs_n, wt, l, ROW_KV, ROW_IK, 1024)
        rest, rest_s = _proj(xn, xs_n, wt, l, lambda j: ROW_REST + j * tn_rest,
                             n_rest // tn_rest, tn_rest, 1024)

        ya = _dsa_prompt(qiq, kvi_b, kvi_f, rest, batch, seq)
        yb, yc, zt = _bc_prompt(rest, conv_w[l], sgu_ln_g[l], sgu_ln_b[l], sgu_w[l], sgu_b[l],
                                batch, seq, min(512, seq))
        outs["kp"].append(kvi_f[:, 0:KV].reshape(batch, seq // PAGE_SIZE, PAGE_SIZE,
                                                 N_KV_HEADS, HEAD_DIM))
        outs["vp"].append(kvi_f[:, KV:2 * KV].reshape(batch, seq // PAGE_SIZE, PAGE_SIZE,
                                                      N_KV_HEADS, HEAD_DIM))
        outs["ikp"].append(kvi_f[:, KVI_IK:KVI_IK + IDX_DIM].reshape(
            batch, seq // PAGE_SIZE, PAGE_SIZE, IDX_DIM))
        outs["cp"].append(zt.reshape(batch, 8, W_B)[:, 8 - (CONV_W - 1):])

        q3 = qiq_s[:, :W_A].reshape(nb, N_HEADS, HEAD_DIM)
        iq3 = qiq_s[:, W_A:].reshape(nb, IDX_HEADS, IDX_DIM)
        kn = kvi_s[:, 0:KV]
        vn = kvi_s[:, KV:2 * KV]
        ikn = kvi_s[:, KVI_IK:KVI_IK + IDX_DIM]
        w3 = kvi_s[:, KVI_IW:KVI_IW + IDX_HEADS].reshape(nb, IDX_HEADS, 1)
        scores = _sample_scores(l, page_table, cache_ik_t, iq3, w3)
        sel, selself = _sample_select(scores, iq3, ikn.reshape(nb, 1, IDX_DIM), w3, topk_s)
        att = _sample_attn(l, page_table, cache_k4, cache_v4, sel, selself, q3,
                           kn.reshape(nb, N_KV_HEADS, HEAD_DIM),
                           vn.reshape(nb, N_KV_HEADS, HEAD_DIM))
        gdim = W_C // C_GROUPS
        w0 = jnp.repeat(sgu_w[l][:, 0, 0], gdim).reshape(1, W_C)
        b0 = jnp.repeat(sgu_b[l][:, 0], gdim).reshape(1, W_C)
        ya_s, yb_s, yc_s, z_s, vn_s = _sample_mid(
            att.reshape(nb, W_A), rest_s, state_conv[l][:, 0], state_conv[l][:, 1],
            conv_w[l], sgu_ln_g[l], sgu_ln_b[l], w0, b0)

        mm, mm_s = _merge((ya, yb, yc), (ya_s, yb_s, yc_s), w_branch, l, rest, rest_s,
                          d, 1024, min(512, d))
        xp, xs = _outproj(mm, mm_s, w_out, l, xp, xs, g_post[l], 512)
        outs["ks"].append(kn.reshape(nb, 1, N_KV_HEADS, HEAD_DIM))
        outs["vs"].append(vn.reshape(nb, 1, N_KV_HEADS, HEAD_DIM))
        outs["iks"].append(ikn.reshape(nb, 1, IDX_DIM))
        outs["cs"].append(jnp.stack([state_conv[l][:, 1], z_s], axis=1))
        outs["chs"].append(vn_s.reshape(nb, 1, W_C))

    st = lambda name: jnp.stack(outs[name])
    return (xp.reshape(batch, seq, d), xs.reshape(nb, 1, d),
            st("kp"), st("vp"), st("ikp"), st("ks"), st("vs"), st("iks"),
            st("cp"), st("cs"), st("chs"))
```

```python
import functools

import jax
import jax.numpy as jnp
from jax import lax
from jax.experimental import pallas as pl
from jax.experimental.pallas import tpu as pltpu

N_HEADS = 8
HEAD_DIM = 128
N_KV_HEADS = 2
GROUP = N_HEADS // N_KV_HEADS
W_A = N_HEADS * HEAD_DIM
KV = N_KV_HEADS * HEAD_DIM
IDX_HEADS = 16
IDX_DIM = 64
TOPK_MAX = 256
Q_BLOCK = 256
PAGE_SIZE = 128
W_B = 1024
CONV_W = 3
W_C = 1024
CHUNK = 128
C_GROUPS = 8
EPS = 1e-6
NEG = -1e30
BIG = 3e38

LANE = 128
KEY_CHUNK = 256
PAGES_PER_STEP = 32
INT_MIN = -(2 ** 31)
INT_MAX = 2 ** 31 - 1
INT16_MAX = 2 ** 15 - 1
LOG2_E = 1.4426950408889634
BF16_SUBLANES = 16
VT_ROWS = HEAD_DIM + BF16_SUBLANES
COUNT_ROWS = 64
WT_CHUNK = 256
PROJ_TM = 2048

ROW_Q = 0
ROW_KV = ROW_Q + W_A
ROW_IQ = ROW_KV + 2 * KV
ROW_IK = ROW_IQ + IDX_HEADS * IDX_DIM
ROW_REST = ROW_IK + IDX_DIM + IDX_HEADS

KVI_W = KV + KV + 3 * LANE
KVI_IK = 2 * KV
KVI_IW = 2 * KV + 2 * LANE

F32 = jnp.float32
BF16 = jnp.bfloat16


def _silu(x):
    return x / (1.0 + jnp.exp(-x))


def _sigmoid(x):
    return 1.0 / (1.0 + jnp.exp(-x))


def _dot_nt(a, b):
    return lax.dot_general(a, b, (((1,), (1,)), ((), ())), preferred_element_type=F32)


def _ordered_key(x):
    bits = pltpu.bitcast(x, jnp.int32)
    return jnp.where(bits < 0, bits ^ jnp.int32(0x7FFFFFFF), bits)


def _key_to_float(key):
    bits = jnp.where(key < 0, key ^ jnp.int32(0x7FFFFFFF), key)
    return pltpu.bitcast(bits, F32)


def _kth_largest_key(count_ge, shape, k, n_total, count_hi=None, make_count_lo=None):
    if count_hi is None:
        count_coarse, n_coarse = count_ge, 0
    else:
        count_coarse, n_coarse = (lambda c: count_hi((c >> 16).astype(jnp.int16))), 15
    zero = jnp.zeros(shape, jnp.int32)
    c0 = count_coarse(zero)
    t = jnp.where(c0 >= k, zero, jnp.full(shape, INT_MIN, jnp.int32))
    cnt_t = jnp.where(c0 >= k, c0, n_total)

    def step(count):
        def body(it, carry):
            t, cnt_t = carry
            cand = t | jnp.left_shift(jnp.int32(1), jnp.int32(30) - it)
            cnt = count(cand)
            keep = cnt >= k
            return jnp.where(keep, cand, t), jnp.where(keep, cnt, cnt_t)
        return body

    carry = lax.fori_loop(0, n_coarse, step(count_coarse), (t, cnt_t))
    count_fine = count_ge if make_count_lo is None else make_count_lo(carry[0])
    return lax.fori_loop(n_coarse, 31, step(count_fine), carry)


def _rmsnorm_kernel(x_ref, g_ref, o_ref):
    x = x_ref[...]
    ms = jnp.mean(x * x, axis=-1, keepdims=True)
    o_ref[...] = (x * lax.rsqrt(ms + EPS) * g_ref[...]).astype(o_ref.dtype)


def _rmsnorm(x, g, out_dtype, tm):
    m, d = x.shape
    tm = min(tm, m)
    return pl.pallas_call(
        _rmsnorm_kernel,
        grid=(m // tm,),
        in_specs=[pl.BlockSpec((tm, d), lambda i: (i, 0)), pl.BlockSpec((1, d), lambda i: (0, 0))],
        out_specs=pl.BlockSpec((tm, d), lambda i: (i, 0)),
        out_shape=jax.ShapeDtypeStruct((m, d), out_dtype),
        compiler_params=pltpu.CompilerParams(dimension_semantics=("parallel",)),
        name="rmsnorm",
    )(x, g.reshape(1, d))


def _load_transposed(wt_ref, wb_ref, col0=0):
    n, d = wt_ref.shape
    for kb in range(d // WT_CHUNK):
        ks = slice(kb * WT_CHUNK, (kb + 1) * WT_CHUNK)
        wb_ref[ks, col0:col0 + n] = wt_ref[:, ks].T.astype(BF16)


def _proj_kernel(xn_ref, xs_ref, wt_ref, o_ref, os_ref, wb_ref):
    @pl.when(pl.program_id(1) == 0)
    def _():
        _load_transposed(wt_ref, wb_ref)
        os_ref[...] = jnp.dot(xs_ref[...].astype(BF16), wb_ref[...], preferred_element_type=F32)

    o_ref[...] = jnp.dot(xn_ref[...], wb_ref[...],
                         preferred_element_type=F32).astype(o_ref.dtype)


def _proj(xn, xs_n, wt, layer, row_of_tile, n_tiles, tn, tm):
    m, d = xn.shape
    nbs = xs_n.shape[0]
    tm = min(tm, m)
    return pl.pallas_call(
        _proj_kernel,
        grid=(n_tiles, m // tm),
        in_specs=[
            pl.BlockSpec((tm, d), lambda j, i: (i, 0)),
            pl.BlockSpec((nbs, d), lambda j, i: (0, 0)),
            pl.BlockSpec((None, pl.Element(tn), pl.Element(d)),
                         lambda j, i: (layer, pl.multiple_of(row_of_tile(j), 8), 0)),
        ],
        out_specs=[pl.BlockSpec((tm, tn), lambda j, i: (i, j)),
                   pl.BlockSpec((nbs, tn), lambda j, i: (0, j))],
        out_shape=[jax.ShapeDtypeStruct((m, n_tiles * tn), BF16),
                   jax.ShapeDtypeStruct((nbs, n_tiles * tn), F32)],
        scratch_shapes=[pltpu.VMEM((d, tn), BF16)],
        compiler_params=pltpu.CompilerParams(
            dimension_semantics=("arbitrary", "arbitrary"),
            vmem_limit_bytes=56 * 1024 * 1024),
        name="proj",
    )(xn, xs_n, wt)


def _proj_kvi_kernel(xn_ref, xs_ref, wkv_ref, wix_ref, of_ref, ob_ref, os_ref, wb_ref):
    @pl.when(pl.program_id(0) == 0)
    def _():
        _load_transposed(wkv_ref, wb_ref)
        d = wb_ref.shape[0]
        lane = lax.broadcasted_iota(jnp.int32, (WT_CHUNK, LANE), 1)
        for kb in range(d // WT_CHUNK):
            ks = slice(kb * WT_CHUNK, (kb + 1) * WT_CHUNK)
            t = wix_ref[:, ks].T
            r = pltpu.roll(t, LANE // 2, 1)
            wb_ref[ks, KVI_IK:KVI_IK + LANE] = jnp.where(lane < IDX_DIM, t, 0.0).astype(BF16)
            wb_ref[ks, KVI_IK + LANE:KVI_IW] = jnp.where(lane >= IDX_DIM, r, 0.0).astype(BF16)
            wb_ref[ks, KVI_IW:KVI_W] = jnp.where(lane < IDX_HEADS, r, 0.0).astype(BF16)
        os_ref[...] = jnp.dot(xs_ref[...].astype(BF16), wb_ref[...], preferred_element_type=F32)

    acc = jnp.dot(xn_ref[...], wb_ref[...], preferred_element_type=F32)
    of_ref[...] = acc
    ob_ref[...] = acc.astype(BF16)


def _proj_kvi(xn, xs_n, wt, layer, kv_row, ik_row, tm):
    m, d = xn.shape
    nbs = xs_n.shape[0]
    tm = min(tm, m)
    return pl.pallas_call(
        _proj_kvi_kernel,
        grid=(m // tm,),
        in_specs=[
            pl.BlockSpec((tm, d), lambda i: (i, 0)),
            pl.BlockSpec((nbs, d), lambda i: (0, 0)),
            pl.BlockSpec((None, pl.Element(2 * KV), pl.Element(d)), lambda i: (layer, kv_row, 0)),
            pl.BlockSpec((None, pl.Element(LANE), pl.Element(d)), lambda i: (layer, ik_row, 0)),
        ],
        out_specs=[pl.BlockSpec((tm, KVI_W), lambda i: (i, 0)),
                   pl.BlockSpec((tm, KVI_W), lambda i: (i, 0)),
                   pl.BlockSpec((nbs, KVI_W), lambda i: (0, 0))],
        out_shape=[jax.ShapeDtypeStruct((m, KVI_W), F32),
                   jax.ShapeDtypeStruct((m, KVI_W), BF16),
                   jax.ShapeDtypeStruct((nbs, KVI_W), F32)],
        scratch_shapes=[pltpu.VMEM((d, KVI_W), BF16)],
        compiler_params=pltpu.CompilerParams(
            dimension_semantics=("arbitrary",),
            vmem_limit_bytes=56 * 1024 * 1024),
        name="proj_kvi",
    )(xn, xs_n, wt, wt)


def _transpose_bf16(x):
    return x.astype(F32).T.astype(BF16)


def _dsa_prompt_kernel(q_ref, iq_ref, w_ref, k_ref, v_ref, ik_ref, ga_ref, o_ref,
                       st_ref, key_ref, khi_ref, klo_ref, iqt_ref, qgt_ref, wt_ref, vt_ref,
                       m0_ref, m1_ref, acc0_ref, acc1_ref, sba_ref, sbb_ref, *, topk):
    i = pl.program_id(1)
    qb = q_ref.shape[0]
    seq = k_ref.shape[0]
    kc = KEY_CHUNK
    nk = (i + 1) * qb
    nch = (nk + kc - 1) // kc
    qpos = i * qb + lax.broadcasted_iota(jnp.int32, (kc, qb), 1)
    krow = lax.broadcasted_iota(jnp.int32, (kc, qb), 0)

    @pl.when(i == 0)
    def _():
        def body(c, carry):
            off = pl.multiple_of(c * kc, kc)
            vt = _transpose_bf16(v_ref[pl.ds(off, kc), :])
            for g in range(N_KV_HEADS):
                vt_ref[g, 0:HEAD_DIM, pl.ds(off, kc)] = vt[g * HEAD_DIM:(g + 1) * HEAD_DIM]
                vt_ref[g, HEAD_DIM:VT_ROWS, pl.ds(off, kc)] = jnp.ones(
                    (VT_ROWS - HEAD_DIM, kc), BF16)
            return carry
        lax.fori_loop(0, seq // kc, body, 0)

    n_tiles = IDX_HEADS // 2
    for j in range(n_tiles):
        iqt_ref[:, j * qb:(j + 1) * qb] = _transpose_bf16(iq_ref[:, j * LANE:(j + 1) * LANE])
    for g in range(N_KV_HEADS):
        for hl in range(GROUP):
            h = g * GROUP + hl
            qgt_ref[g, :, hl * qb:(hl + 1) * qb] = _transpose_bf16(
                q_ref[:, h * HEAD_DIM:(h + 1) * HEAD_DIM])
    wt_ref[...] = (w_ref[...] * ((IDX_HEADS ** -0.5) * (IDX_DIM ** -0.5))).T

    def score_chunk(c):
        off = pl.multiple_of(c * kc, kc)
        acc = jnp.zeros((kc, qb), F32)
        for half in range(2):
            res = jnp.dot(ik_ref[pl.ds(off, kc), half * LANE:(half + 1) * LANE], iqt_ref[...],
                          preferred_element_type=F32)
            for j in range(n_tiles):
                h = 2 * j + half
                acc = acc + jnp.maximum(res[:, j * qb:(j + 1) * qb], 0.0) * wt_ref[h:h + 1, :]
        sc = jnp.where(off + krow <= qpos, acc, NEG)
        key = _ordered_key(sc)
        key_ref[pl.ds(off, kc), :] = key
        khi_ref[pl.ds(off, kc), :] = (key >> 16).astype(jnp.int16)

    def score_body(c2, carry):
        score_chunk(2 * c2)
        score_chunk(2 * c2 + 1)
        return carry

    lax.fori_loop(0, (nch + 1) // 2, score_body, 0)

    def count_ge(cand):
        def body(c, acc):
            off = pl.multiple_of(c * kc, kc)
            ge = jnp.where(key_ref[pl.ds(off, kc), :] >= cand, 1.0, 0.0)
            return acc + jnp.sum(ge.reshape(kc // COUNT_ROWS, COUNT_ROWS, qb), axis=0)
        acc = lax.fori_loop(0, nch, body, jnp.zeros((COUNT_ROWS, qb), F32))
        return jnp.sum(acc, axis=0, keepdims=True)

    def count16(ref, cand16):
        rows = 2 * COUNT_ROWS
        def body(c, acc):
            off = pl.multiple_of(c * kc, kc)
            ge = jnp.where(ref[pl.ds(off, kc), :] >= cand16, jnp.int16(1), jnp.int16(0))
            for r in range(kc // rows):
                acc = acc + ge[r * rows:(r + 1) * rows]
            return acc
        acc = lax.fori_loop(0, nch, body, jnp.zeros((rows, qb), jnp.int16))
        return jnp.sum(acc.astype(F32), axis=0, keepdims=True)

    def make_count_lo(t_coarse):
        hi = t_coarse >> 16
        hi16 = hi.astype(jnp.int16)
        above = jnp.where(hi >= INT16_MAX, 0.0,
                          count16(khi_ref, jnp.minimum(hi + 1, INT16_MAX).astype(jnp.int16)))

        def body(c, carry):
            off = pl.multiple_of(c * kc, kc)
            lo = ((key_ref[pl.ds(off, kc), :] & 0xFFFF) - 0x8000).astype(jnp.int16)
            same = khi_ref[pl.ds(off, kc), :] == hi16
            klo_ref[pl.ds(off, kc), :] = jnp.where(same, lo, jnp.int16(-0x8000))
            return carry
        lax.fori_loop(0, nch, body, 0)
        return lambda cand: above + count16(
            klo_ref, ((cand & 0xFFFF) - 0x8000).astype(jnp.int16))

    t_key, cnt_ge = _kth_largest_key(
        count_ge, (1, qb), float(topk), (nch * kc).astype(F32),
        count_hi=functools.partial(count16, khi_ref), make_count_lo=make_count_lo)

    at_top = t_key == INT_MAX
    cnt_gt = jnp.where(at_top, 0.0, count_ge(jnp.where(at_top, t_key, t_key + 1)))
    need = float(topk) - cnt_gt
    excess_ties = jnp.max(cnt_ge - cnt_gt - need) > 0.0

    def write_bounds(selected):
        def body(c, carry):
            off = pl.multiple_of(c * kc, kc)
            kidx = off + krow
            bound = jnp.where(selected(key_ref[pl.ds(off, kc), :], kidx), BIG, NEG)
            st_ref[pl.ds(off, kc), :] = jnp.where(kidx <= qpos, bound, NEG)
            return carry
        lax.fori_loop(0, nch, body, 0)

    @pl.when(jnp.logical_not(excess_ties))
    def _():
        write_bounds(lambda kk, kidx: kk >= t_key)

    @pl.when(excess_ties)
    def _():
        def tied_before(p):
            def body(c, acc):
                off = pl.multiple_of(c * kc, kc)
                hit = (key_ref[pl.ds(off, kc), :] == t_key) & (off + krow < p)
                hit = jnp.where(hit, 1.0, 0.0)
                return acc + jnp.sum(hit.reshape(kc // COUNT_ROWS, COUNT_ROWS, qb), axis=0)
            acc = lax.fori_loop(0, nch, body, jnp.zeros((COUNT_ROWS, qb), F32))
            return jnp.sum(acc, axis=0, keepdims=True)

        n_bits = max(1, (seq - 1).bit_length())

        def body(it, p):
            cand = p | jnp.left_shift(jnp.int32(1), jnp.int32(n_bits - 1) - it)
            return jnp.where(tied_before(cand) < need, cand, p)

        cut = lax.fori_loop(0, n_bits, body, jnp.zeros((1, qb), jnp.int32))
        write_bounds(lambda kk, kidx: (kk > t_key) | ((kk == t_key) & (kidx <= cut)))

    exp2_scale = (HEAD_DIM ** -0.5) * LOG2_E
    m_refs = (m0_ref, m1_ref)
    acc_refs = (acc0_ref, acc1_ref)
    for g in range(N_KV_HEADS):
        m_refs[g][...] = jnp.full(m_refs[g].shape, -jnp.inf, F32)
        acc_refs[g][...] = jnp.zeros(acc_refs[g].shape, F32)

    def logits(c, dst_ref):
        off = pl.multiple_of(jnp.minimum(c, nch - 1) * kc, kc)
        for g in range(N_KV_HEADS):
            dst_ref[g] = jnp.dot(
                k_ref[pl.ds(off, kc), g * HEAD_DIM:(g + 1) * HEAD_DIM], qgt_ref[g],
                preferred_element_type=F32)

    def softmax_pv(c, src_ref):
        off = pl.multiple_of(jnp.minimum(c, nch - 1) * kc, kc)
        bound = jnp.where(c < nch, st_ref[pl.ds(off, kc), :], NEG)
        bound = jnp.concatenate([bound] * GROUP, axis=1)
        for g in range(N_KV_HEADS):
            s = jnp.minimum(src_ref[g], bound)
            m_prev = m_refs[g][...]
            m_new = jnp.maximum(m_prev, jnp.max(s, axis=0, keepdims=True))
            alpha = jnp.exp2((m_prev - m_new) * exp2_scale)
            p = jnp.exp2((s - m_new) * exp2_scale)
            acc_refs[g][...] = alpha * acc_refs[g][...] + jnp.dot(
                vt_ref[g, :, pl.ds(off, kc)], p.astype(BF16), preferred_element_type=F32)
            m_refs[g][...] = m_new

    logits(0, sba_ref)

    def attn_body(c2, carry):
        c = 2 * c2
        logits(c + 1, sbb_ref)
        softmax_pv(c, sba_ref)
        logits(c + 2, sba_ref)
        softmax_pv(c + 1, sbb_ref)
        return carry

    lax.fori_loop(0, (nch + 1) // 2, attn_body, 0)
    for g in range(N_KV_HEADS):
        acc = acc_refs[g][...]
        o_t = acc[0:HEAD_DIM] / acc[HEAD_DIM:HEAD_DIM + 1]
        for hl in range(GROUP):
            h = g * GROUP + hl
            cols = slice(h * HEAD_DIM, (h + 1) * HEAD_DIM)
            gate = _silu(ga_ref[:, cols].astype(F32))
            o_ref[:, cols] = (o_t[:, hl * qb:(hl + 1) * qb].T * gate).astype(o_ref.dtype)


def _dsa_prompt(qiq, kvi_b, kvi_f, rest, batch, seq):
    m = qiq.shape[0]
    qb = Q_BLOCK
    nq = seq // qb
    topk = min(TOPK_MAX, seq // 4)
    assert seq % (2 * KEY_CHUNK) == 0 and topk <= KEY_CHUNK
    row = lambda b, i: b * nq + i
    return pl.pallas_call(
        functools.partial(_dsa_prompt_kernel, topk=topk),
        grid=(batch, nq),
        in_specs=[
            pl.BlockSpec((qb, W_A), lambda b, i: (row(b, i), 0)),
            pl.BlockSpec((qb, IDX_HEADS * IDX_DIM), lambda b, i: (row(b, i), 1)),
            pl.BlockSpec((qb, LANE), lambda b, i: (row(b, i), KVI_IW // LANE)),
            pl.BlockSpec((seq, KV), lambda b, i: (b, 0)),
            pl.BlockSpec((seq, KV), lambda b, i: (b, 1)),
            pl.BlockSpec((seq, 2 * LANE), lambda b, i: (b, KVI_IK // (2 * LANE))),
            pl.BlockSpec((qb, W_A), lambda b, i: (row(b, i), 0)),
        ],
        out_specs=pl.BlockSpec((qb, W_A), lambda b, i: (row(b, i), 0)),
        out_shape=jax.ShapeDtypeStruct((m, W_A), BF16),
        scratch_shapes=[
            pltpu.VMEM((seq, qb), F32),
            pltpu.VMEM((seq, qb), jnp.int32),
            pltpu.VMEM((seq, qb), jnp.int16),
            pltpu.VMEM((seq, qb), jnp.int16),
            pltpu.VMEM((LANE, IDX_HEADS // 2 * qb), BF16),
            pltpu.VMEM((N_KV_HEADS, HEAD_DIM, GROUP * qb), BF16),
            pltpu.VMEM((LANE, qb), F32),
            pltpu.VMEM((N_KV_HEADS, VT_ROWS, seq), BF16),
            pltpu.VMEM((1, GROUP * qb), F32),
            pltpu.VMEM((1, GROUP * qb), F32),
            pltpu.VMEM((VT_ROWS, GROUP * qb), F32),
            pltpu.VMEM((VT_ROWS, GROUP * qb), F32),
            pltpu.VMEM((N_KV_HEADS, KEY_CHUNK, GROUP * qb), F32),
            pltpu.VMEM((N_KV_HEADS, KEY_CHUNK, GROUP * qb), F32),
        ],
        compiler_params=pltpu.CompilerParams(
            dimension_semantics=("parallel", "arbitrary"),
            vmem_limit_bytes=56 * 1024 * 1024),
        name="dsa_prompt",
    )(qiq, qiq, kvi_f, kvi_b, kvi_b, kvi_b, rest)


PREV_ROWS = 16


def _bc_prompt_kernel(h_ref, bg_ref, cg_ref, gb_ref, u_ref, vc_ref, gc_ref, ph_ref, pcg_ref,
                      cw_ref, lg_ref, lb_ref, ws_ref, bst_ref,
                      yb_ref, yc_ref, zt_ref, zs_ref):
    tm = h_ref.shape[0]
    j = pl.program_id(1)
    z = cg_ref[...].astype(F32) * h_ref[...].astype(F32)
    pz = pcg_ref[...].astype(F32) * ph_ref[...].astype(F32)
    pz = jnp.where(j == 0, 0.0, pz)
    zs_ref[0:PREV_ROWS, :] = pz
    zs_ref[PREV_ROWS:PREV_ROWS + tm, :] = z
    z1 = zs_ref[PREV_ROWS - 1:PREV_ROWS - 1 + tm, :]
    z2 = zs_ref[PREV_ROWS - 2:PREV_ROWS - 2 + tm, :]
    conv = cw_ref[0:1, :] * z2 + cw_ref[1:2, :] * z1 + cw_ref[2:3, :] * z
    yb = bg_ref[...].astype(F32) * conv * _silu(gb_ref[...].astype(F32))
    yb_ref[...] = yb.astype(yb_ref.dtype)
    zt_ref[...] = z[tm - 8:tm]

    vc = vc_ref[...].astype(F32)
    mu = jnp.mean(vc, axis=-1, keepdims=True)
    dv = vc - mu
    var = jnp.mean(dv * dv, axis=-1, keepdims=True)
    vn = (dv * lax.rsqrt(var + EPS) * lg_ref[...] + lb_ref[...]).astype(BF16)
    rr = lax.broadcasted_iota(jnp.int32, (CHUNK, CHUNK), 0)
    cc = lax.broadcasted_iota(jnp.int32, (CHUNK, CHUNK), 1)
    gdim = W_C // C_GROUPS
    for g in range(C_GROUPS):
        wg = jnp.where(rr >= cc, ws_ref[g], 0.0).astype(BF16)
        bias = bst_ref[:, g:g + 1]
        cols = slice(g * gdim, (g + 1) * gdim)
        for c in range(tm // CHUNK):
            rows = slice(c * CHUNK, (c + 1) * CHUNK)
            mixed = jnp.dot(wg, vn[rows, cols], preferred_element_type=F32) + bias
            yc = u_ref[rows, cols].astype(F32) * mixed * _silu(gc_ref[rows, cols].astype(F32))
            yc_ref[rows, cols] = yc.astype(yc_ref.dtype)


def _bc_prompt(rest, conv_w, ln_g, ln_b, ws, bs, batch, seq, tm):
    m = rest.shape[0]
    nt = seq // tm
    row = lambda b, j: b * nt + j
    prev = lambda b, j: jnp.maximum((b * seq + j * tm) // PREV_ROWS - 1, 0)
    col = lambda c: (lambda b, j: (row(b, j), c))
    pcol = lambda c: (lambda b, j: (prev(b, j), c))
    const2 = lambda b, j: (0, 0)
    yb, yc, zt = pl.pallas_call(
        _bc_prompt_kernel,
        grid=(batch, nt),
        in_specs=[pl.BlockSpec((tm, W_B), col(c)) for c in range(1, 8)] + [
            pl.BlockSpec((PREV_ROWS, W_B), pcol(1)),
            pl.BlockSpec((PREV_ROWS, W_B), pcol(3)),
            pl.BlockSpec((CONV_W, W_B), const2),
            pl.BlockSpec((1, W_C), const2),
            pl.BlockSpec((1, W_C), const2),
            pl.BlockSpec((C_GROUPS, CHUNK, CHUNK), lambda b, j: (0, 0, 0)),
            pl.BlockSpec((CHUNK, C_GROUPS), const2),
        ],
        out_specs=[
            pl.BlockSpec((tm, W_B), lambda b, j: (row(b, j), 0)),
            pl.BlockSpec((tm, W_C), lambda b, j: (row(b, j), 0)),
            pl.BlockSpec((8, W_B), lambda b, j: (b, 0)),
        ],
        out_shape=[
            jax.ShapeDtypeStruct((m, W_B), BF16),
            jax.ShapeDtypeStruct((m, W_C), BF16),
            jax.ShapeDtypeStruct((batch * 8, W_B), F32),
        ],
        scratch_shapes=[pltpu.VMEM((tm + PREV_ROWS, W_B), F32)],
        compiler_params=pltpu.CompilerParams(
            dimension_semantics=("parallel", "arbitrary"),
            vmem_limit_bytes=56 * 1024 * 1024),
        name="bc_prompt",
    )(*([rest] * 7), rest, rest, conv_w, ln_g.reshape(1, W_C), ln_b.reshape(1, W_C), ws, bs.T)
    return yb, yc, zt


N_BRANCH = 3


def _merge_kernel(*refs):
    y_refs, ys_refs = refs[0:3], refs[3:6]
    w_refs = refs[6:9]
    r_refs, rs_refs = refs[9:12], refs[12:15]
    o_ref, os_ref, wbf_ref = refs[15:18]

    def merged(ys, rs):
        out = None
        for b in range(N_BRANCH):
            proj = jnp.dot(ys[b][...].astype(BF16), wbf_ref[b], preferred_element_type=F32)
            term = _sigmoid(rs[b][...].astype(F32)) * proj
            out = term if out is None else out + term
        return out

    @pl.when(pl.program_id(1) == 0)
    def _():
        for b in range(N_BRANCH):
            wbf_ref[b] = w_refs[b][...].astype(BF16)
        os_ref[...] = merged(ys_refs, rs_refs)

    o_ref[...] = merged(y_refs, r_refs).astype(o_ref.dtype)


def _merge(ys, ys_s, w_branch, layer, rest, rest_s, d, tm, tn):
    m = ys[0].shape[0]
    nbs = ys_s[0].shape[0]
    tm = min(tm, m)
    nb = d // tn
    r0 = 8 * W_A // tn
    y_spec = pl.BlockSpec((tm, W_A), lambda j, i: (i, 0))
    ys_spec = pl.BlockSpec((nbs, W_A), lambda j, i: (0, 0))
    w_spec = lambda b: pl.BlockSpec((None, W_A, tn), lambda j, i: (layer, b, j))
    r_spec = lambda b: pl.BlockSpec((tm, tn), lambda j, i: (i, r0 + b * nb + j))
    rs_spec = lambda b: pl.BlockSpec((nbs, tn), lambda j, i: (0, r0 + b * nb + j))
    rng = range(N_BRANCH)
    return pl.pallas_call(
        _merge_kernel,
        grid=(nb, m // tm),
        in_specs=([y_spec] * 3 + [ys_spec] * 3 + [w_spec(b) for b in rng]
                  + [r_spec(b) for b in rng] + [rs_spec(b) for b in rng]),
        out_specs=[pl.BlockSpec((tm, tn), lambda j, i: (i, j)),
                   pl.BlockSpec((nbs, tn), lambda j, i: (0, j))],
        out_shape=[jax.ShapeDtypeStruct((m, d), BF16), jax.ShapeDtypeStruct((nbs, d), F32)],
        scratch_shapes=[pltpu.VMEM((N_BRANCH, W_A, tn), BF16)],
        compiler_params=pltpu.CompilerParams(
            dimension_semantics=("arbitrary", "arbitrary"),
            vmem_limit_bytes=56 * 1024 * 1024),
        name="merge",
    )(*ys, *ys_s, *([w_branch] * 3), *([rest] * 3), *([rest_s] * 3))


def _outproj_kernel(m_ref, ms_ref, w_ref, x_ref, xs_ref, g_ref, o_ref, os_ref, wbf_ref):
    def residual(m, x):
        out = jnp.dot(m.astype(BF16), wbf_ref[...], preferred_element_type=F32)
        ms = jnp.mean(out * out, axis=-1, keepdims=True)
        return x + out * lax.rsqrt(ms + EPS) * g_ref[...]

    @pl.when(pl.program_id(0) == 0)
    def _():
        wbf_ref[...] = w_ref[...].astype(BF16)
        os_ref[...] = residual(ms_ref[...], xs_ref[...])

    o_ref[...] = residual(m_ref[...], x_ref[...])


def _outproj(mm, mm_s, w_out, layer, x, xs, g, tm):
    m, d = x.shape
    nbs = xs.shape[0]
    tm = min(tm, m)
    row_spec = pl.BlockSpec((tm, d), lambda i: (i, 0))
    s_spec = pl.BlockSpec((nbs, d), lambda i: (0, 0))
    return pl.pallas_call(
        _outproj_kernel,
        grid=(m // tm,),
        in_specs=[
            row_spec, s_spec,
            pl.BlockSpec((None, d, d), lambda i: (layer, 0, 0), pipeline_mode=pl.Buffered(1)),
            row_spec, s_spec,
            pl.BlockSpec((1, d), lambda i: (0, 0)),
        ],
        out_specs=[row_spec, s_spec],
        out_shape=[jax.ShapeDtypeStruct((m, d), F32), jax.ShapeDtypeStruct((nbs, d), F32)],
        scratch_shapes=[pltpu.VMEM((d, d), BF16)],
        compiler_params=pltpu.CompilerParams(
            dimension_semantics=("arbitrary",),
            vmem_limit_bytes=56 * 1024 * 1024),
        name="outproj",
    )(mm, mm_s, w_out, x, xs, g.reshape(1, d))


def _sample_scores_kernel(pt_ref, *refs):
    del pt_ref
    npg = PAGES_PER_STEP
    page_refs = refs[:npg]
    iq_ref, w_ref, o_ref = refs[npg:]
    iq = iq_ref[...].astype(BF16)
    w = w_ref[...] * ((IDX_HEADS ** -0.5) * (IDX_DIM ** -0.5))
    for u in range(npg):
        s = jnp.dot(iq, page_refs[u][...].astype(BF16),
                    preferred_element_type=F32)
        o_ref[u:u + 1, :] = jnp.sum(jnp.maximum(s, 0.0) * w, axis=0, keepdims=True)


def _sample_scores(layer, page_table, cache_ik_t, iq3, w3):
    nb, n_pages = page_table.shape
    npg = PAGES_PER_STEP
    assert n_pages % npg == 0
    page_spec = lambda u: pl.BlockSpec(
        (None, None, IDX_DIM, PAGE_SIZE),
        lambda s, j, pt: (layer, pt[s, j * npg + u], 0, 0))
    grid_spec = pltpu.PrefetchScalarGridSpec(
        num_scalar_prefetch=1,
        grid=(nb, n_pages // npg),
        in_specs=[page_spec(u) for u in range(npg)] + [
            pl.BlockSpec((None, IDX_HEADS, IDX_DIM), lambda s, j, pt: (s, 0, 0)),
            pl.BlockSpec((None, IDX_HEADS, 1), lambda s, j, pt: (s, 0, 0)),
        ],
        out_specs=pl.BlockSpec((None, npg, PAGE_SIZE), lambda s, j, pt: (s, j, 0)),
    )
    return pl.pallas_call(
        _sample_scores_kernel,
        grid_spec=grid_spec,
        out_shape=jax.ShapeDtypeStruct((nb, n_pages, PAGE_SIZE), F32),
        compiler_params=pltpu.CompilerParams(dimension_semantics=("parallel", "arbitrary")),
        name="sample_scores",
    )(page_table, *([cache_ik_t] * npg), iq3, w3)


def _sample_select_kernel(sc_ref, iq_ref, ikn_ref, w_ref, sel_ref, selself_ref, *, topk):
    nb = sc_ref.shape[0]
    w = w_ref[...] * ((IDX_HEADS ** -0.5) * (IDX_DIM ** -0.5))
    s_self = jnp.sum(iq_ref[...] * ikn_ref[...], axis=-1, keepdims=True)
    sc_self = jnp.sum(jnp.maximum(s_self, 0.0) * w, axis=1, keepdims=True)
    sc = sc_ref[...]
    key = _ordered_key(sc)
    key_self = _ordered_key(sc_self)

    def count_ge(cand):
        ge = jnp.where(key >= cand, 1.0, 0.0)
        cnt = jnp.sum(jnp.sum(ge, axis=2, keepdims=True), axis=1, keepdims=True)
        return cnt + jnp.where(key_self >= cand, 1.0, 0.0)

    n_past = sc.shape[1] * sc.shape[2]
    t_key, _ = _kth_largest_key(count_ge, (nb, 1, 1), float(topk), float(n_past + 1))

    at_top = t_key == INT_MAX
    cnt_gt = jnp.where(at_top, 0.0, count_ge(jnp.where(at_top, t_key, t_key + 1)))
    need = float(topk) - cnt_gt
    pos = (lax.broadcasted_iota(jnp.int32, sc.shape, 1) * sc.shape[2]
           + lax.broadcasted_iota(jnp.int32, sc.shape, 2))
    tied = key == t_key

    def tied_before(p):
        hit = jnp.where(tied & (pos < p), 1.0, 0.0)
        return jnp.sum(jnp.sum(hit, axis=2, keepdims=True), axis=1, keepdims=True)

    n_bits = n_past.bit_length()

    def body(it, p):
        cand = p | jnp.left_shift(jnp.int32(1), jnp.int32(n_bits - 1) - it)
        return jnp.where(tied_before(cand) < need, cand, p)

    cut = lax.fori_loop(0, n_bits, body, jnp.zeros((nb, 1, 1), jnp.int32))
    sel_ref[...] = jnp.where((key > t_key) | (tied & (pos <= cut)), 1.0, 0.0)
    self_sel = (key_self > t_key) | ((key_self == t_key) & (n_past <= cut))
    selself_ref[...] = jnp.broadcast_to(jnp.where(self_sel, 1.0, 0.0), selself_ref.shape)


def _sample_select(scores, iq3, ikn3, w3, topk):
    nb, n_pages, _ = scores.shape
    return pl.pallas_call(
        functools.partial(_sample_select_kernel, topk=topk),
        out_shape=[jax.ShapeDtypeStruct((nb, n_pages, PAGE_SIZE), F32),
                   jax.ShapeDtypeStruct((nb, 1, LANE), F32)],
        name="sample_select",
    )(scores, iq3, ikn3, w3)


def _sample_attn_kernel(pt_ref, *refs):
    del pt_ref
    npg = PAGES_PER_STEP
    k_refs = refs[:npg]
    v_refs = refs[npg:2 * npg]
    sel_ref, selself_ref, q_ref, kn_ref, vn_ref, o_ref, m_ref, l_ref, acc_ref = refs[2 * npg:]
    j = pl.program_id(1)
    scale = HEAD_DIM ** -0.5
    rows2 = N_KV_HEADS * PAGE_SIZE

    @pl.when(j == 0)
    def _():
        m_ref[...] = jnp.full(m_ref.shape, -jnp.inf, F32)
        l_ref[...] = jnp.zeros(l_ref.shape, F32)
        acc_ref[...] = jnp.zeros(acc_ref.shape, F32)

    q = q_ref[...].astype(BF16)
    er = lax.broadcasted_iota(jnp.int32, (PAGE_SIZE, rows2), 0)
    ec = lax.broadcasted_iota(jnp.int32, (PAGE_SIZE, rows2), 1)
    expand = jnp.where(ec // N_KV_HEADS == er, 1.0, 0.0).astype(BF16)
    sel2 = jnp.dot(sel_ref[...].astype(BF16), expand, preferred_element_type=F32)
    hh = lax.broadcasted_iota(jnp.int32, (N_HEADS, rows2), 0)
    cc = lax.broadcasted_iota(jnp.int32, (N_HEADS, rows2), 1)
    head_ok = (hh // GROUP) == (cc % N_KV_HEADS)

    def update(s, v):
        m_prev = m_ref[...]
        m_new = jnp.maximum(m_prev, jnp.max(s, axis=-1, keepdims=True))
        alpha = jnp.exp(m_prev - m_new)
        p = jnp.exp(s - m_new)
        l_ref[...] = alpha * l_ref[...] + jnp.sum(p, axis=-1, keepdims=True)
        acc_ref[...] = alpha * acc_ref[...] + v(p)
        m_ref[...] = m_new

    logits = []
    for u in range(npg):
        s = _dot_nt(q, k_refs[u][...].astype(BF16)) * scale
        logits.append(jnp.where(head_ok & (sel2[u:u + 1, :] > 0.5), s, NEG))

    def weighted_values(p):
        pb = p.astype(BF16)
        out = jnp.zeros((N_HEADS, HEAD_DIM), F32)
        for u in range(npg):
            out = out + jnp.dot(pb[:, u * rows2:(u + 1) * rows2], v_refs[u][...].astype(BF16),
                                preferred_element_type=F32)
        return out

    update(jnp.concatenate(logits, axis=1), weighted_values)

    @pl.when(j == pl.num_programs(1) - 1)
    def _():
        hrow = lax.broadcasted_iota(jnp.int32, (N_HEADS, HEAD_DIM), 0)
        kn = kn_ref[...]
        vn = vn_ref[...]
        k_self = jnp.where(hrow < GROUP, kn[0:1, :], kn[1:2, :])
        v_self = jnp.where(hrow < GROUP, vn[0:1, :], vn[1:2, :])
        s_self = jnp.sum(q_ref[...] * k_self, axis=-1, keepdims=True) * scale
        s_self = jnp.where(selself_ref[0:1, 0:1] > 0.5, s_self, NEG)
        update(s_self, lambda p: p * v_self)
        o_ref[...] = acc_ref[...] / l_ref[...]


def _sample_attn(layer, page_table, cache_k4, cache_v4, sel, selself, q3, kn3, vn3):
    nb, n_pages = page_table.shape
    npg = PAGES_PER_STEP
    rows2 = N_KV_HEADS * PAGE_SIZE
    page_spec = lambda u: pl.BlockSpec(
        (None, None, rows2, HEAD_DIM),
        lambda s, j, pt: (layer, pt[s, j * npg + u], 0, 0))
    per_seq = lambda shape: pl.BlockSpec((None,) + shape, lambda s, j, pt: (s, 0, 0))
    grid_spec = pltpu.PrefetchScalarGridSpec(
        num_scalar_prefetch=1,
        grid=(nb, n_pages // npg),
        in_specs=[page_spec(u) for u in range(npg)] * 2 + [
            pl.BlockSpec((None, npg, PAGE_SIZE), lambda s, j, pt: (s, j, 0)),
            per_seq((1, LANE)),
            per_seq((N_HEADS, HEAD_DIM)),
            per_seq((N_KV_HEADS, HEAD_DIM)),
            per_seq((N_KV_HEADS, HEAD_DIM)),
        ],
        out_specs=per_seq((N_HEADS, HEAD_DIM)),
        scratch_shapes=[
            pltpu.VMEM((N_HEADS, 1), F32),
            pltpu.VMEM((N_HEADS, 1), F32),
            pltpu.VMEM((N_HEADS, HEAD_DIM), F32),
        ],
    )
    return pl.pallas_call(
        _sample_attn_kernel,
        grid_spec=grid_spec,
        out_shape=jax.ShapeDtypeStruct((nb, N_HEADS, HEAD_DIM), F32),
        compiler_params=pltpu.CompilerParams(dimension_semantics=("parallel", "arbitrary")),
        name="sample_attn",
    )(page_table, *([cache_k4] * npg), *([cache_v4] * npg), sel, selself, q3, kn3, vn3)


def _sample_mid_kernel(att_ref, rest_ref, s0_ref, s1_ref, cw_ref, lg_ref, lb_ref, w0_ref, b0_ref,
                       ya_ref, yb_ref, yc_ref, z_ref, vn_ref):
    blk = lambda c: rest_ref[:, c * W_A:(c + 1) * W_A]
    ga, h, bg, cg, gb, u, vc, gc = [blk(c) for c in range(8)]
    ya_ref[...] = att_ref[...] * _silu(ga)
    z = cg * h
    conv = cw_ref[0:1, :] * s0_ref[...] + cw_ref[1:2, :] * s1_ref[...] + cw_ref[2:3, :] * z
    yb_ref[...] = bg * conv * _silu(gb)
    z_ref[...] = z
    mu = jnp.mean(vc, axis=-1, keepdims=True)
    dv = vc - mu
    var = jnp.mean(dv * dv, axis=-1, keepdims=True)
    vn = dv * lax.rsqrt(var + EPS) * lg_ref[...] + lb_ref[...]
    vn_ref[...] = vn
    mixed = w0_ref[...] * vn + b0_ref[...]
    yc_ref[...] = u * mixed * _silu(gc)


def _sample_mid(att, rest, s0, s1, conv_w, ln_g, ln_b, w0, b0):
    nb = att.shape[0]
    shp = jax.ShapeDtypeStruct((nb, W_A), F32)
    return pl.pallas_call(
        _sample_mid_kernel,
        out_shape=[shp] * 5,
        name="sample_mid",
    )(att, rest, s0, s1, conv_w, ln_g.reshape(1, W_C), ln_b.reshape(1, W_C), w0, b0)


def kernel(x_prompt, x_sample, cache_k, cache_v, cache_idx_k, state_conv, page_table,
           g_pre, g_post, w_in, conv_w, sgu_ln_g, sgu_ln_b, sgu_w, sgu_b, w_branch, w_out):
    depth = w_in.shape[0]
    batch, seq, d = x_prompt.shape
    nb = x_sample.shape[0]
    n_pool = cache_k.shape[1]
    n_pages = page_table.shape[1]
    topk_s = min(TOPK_MAX, (n_pages * PAGE_SIZE + 1) // 4)
    assert x_sample.shape[1] == 1

    xp = x_prompt.reshape(batch * seq, d)
    xs = x_sample.reshape(nb, d)
    cache_k4 = cache_k.reshape(depth, n_pool, PAGE_SIZE * N_KV_HEADS, HEAD_DIM)
    cache_v4 = cache_v.reshape(depth, n_pool, PAGE_SIZE * N_KV_HEADS, HEAD_DIM)
    cache_ik_t = jnp.swapaxes(cache_idx_k, 2, 3)

    wt = jnp.swapaxes(w_in, 1, 2)
    n_rest = wt.shape[1] - ROW_REST
    tn_rest = 1024
    while n_rest % tn_rest:
        tn_rest //= 2
    assert tn_rest % LANE == 0

    outs = {name: [] for name in ("kp", "vp", "ikp", "ks", "vs", "iks", "cp", "cs", "chs")}
    for l in range(depth):
        xn = _rmsnorm(xp, g_pre[l], BF16, 1024)
        xs_n = _rmsnorm(xs, g_pre[l], F32, 8)
        qiq, qiq_s = _proj(xn, xs_n, wt, l, lambda j: jnp.where(j == 0, ROW_Q, ROW_IQ),
                           2, W_A, PROJ_TM)
        kvi_f, kvi_b, kvi_s = _proj_kvi(xn, xs_n, wt, l, ROW_KV, ROW_IK, 1024)
        rest, rest_s = _proj(xn, xs_n, wt, l, lambda j: ROW_REST + j * tn_rest,
                             n_rest // tn_rest, tn_rest, PROJ_TM)

        ya = _dsa_prompt(qiq, kvi_b, kvi_f, rest, batch, seq)
        yb, yc, zt = _bc_prompt(rest, conv_w[l], sgu_ln_g[l], sgu_ln_b[l], sgu_w[l], sgu_b[l],
                                batch, seq, min(512, seq))
        outs["kp"].append(kvi_f[:, 0:KV].reshape(batch, seq // PAGE_SIZE, PAGE_SIZE,
                                                 N_KV_HEADS, HEAD_DIM))
        outs["vp"].append(kvi_f[:, KV:2 * KV].reshape(batch, seq // PAGE_SIZE, PAGE_SIZE,
                                                      N_KV_HEADS, HEAD_DIM))
        outs["ikp"].append(kvi_f[:, KVI_IK:KVI_IK + IDX_DIM].reshape(
            batch, seq // PAGE_SIZE, PAGE_SIZE, IDX_DIM))
        outs["cp"].append(zt.reshape(batch, 8, W_B)[:, 8 - (CONV_W - 1):])

        q3 = qiq_s[:, :W_A].reshape(nb, N_HEADS, HEAD_DIM)
        iq3 = qiq_s[:, W_A:].reshape(nb, IDX_HEADS, IDX_DIM)
        kn = kvi_s[:, 0:KV]
        vn = kvi_s[:, KV:2 * KV]
        ikn = kvi_s[:, KVI_IK:KVI_IK + IDX_DIM]
        w3 = kvi_s[:, KVI_IW:KVI_IW + IDX_HEADS].reshape(nb, IDX_HEADS, 1)
        scores = _sample_scores(l, page_table, cache_ik_t, iq3, w3)
        sel, selself = _sample_select(scores, iq3, ikn.reshape(nb, 1, IDX_DIM), w3, topk_s)
        att = _sample_attn(l, page_table, cache_k4, cache_v4, sel, selself, q3,
                           kn.reshape(nb, N_KV_HEADS, HEAD_DIM),
                           vn.reshape(nb, N_KV_HEADS, HEAD_DIM))
        gdim = W_C // C_GROUPS
        w0 = jnp.repeat(sgu_w[l][:, 0, 0], gdim).reshape(1, W_C)
        b0 = jnp.repeat(sgu_b[l][:, 0], gdim).reshape(1, W_C)
        ya_s, yb_s, yc_s, z_s, vn_s = _sample_mid(
            att.reshape(nb, W_A), rest_s, state_conv[l][:, 0], state_conv[l][:, 1],
            conv_w[l], sgu_ln_g[l], sgu_ln_b[l], w0, b0)

        mm, mm_s = _merge((ya, yb, yc), (ya_s, yb_s, yc_s), w_branch, l, rest, rest_s,
                          d, 1024, min(512, d))
        xp, xs = _outproj(mm, mm_s, w_out, l, xp, xs, g_post[l], 512)
        outs["ks"].append(kn.reshape(nb, 1, N_KV_HEADS, HEAD_DIM))
        outs["vs"].append(vn.reshape(nb, 1, N_KV_HEADS, HEAD_DIM))
        outs["iks"].append(ikn.reshape(nb, 1, IDX_DIM))
        outs["cs"].append(jnp.stack([state_conv[l][:, 1], z_s], axis=1))
        outs["chs"].append(vn_s.reshape(nb, 1, W_C))

    st = lambda name: jnp.stack(outs[name])
    return (xp.reshape(batch, seq, d), xs.reshape(nb, 1, d),
            st("kp"), st("vp"), st("ikp"), st("ks"), st("vs"), st("iks"),
            st("cp"), st("cs"), st("chs"))
```

```python
import functools

import jax
import jax.numpy as jnp
from jax import lax
from jax.experimental import pallas as pl
from jax.experimental.pallas import tpu as pltpu

N_HEADS = 8
HEAD_DIM = 128
N_KV_HEADS = 2
GROUP = N_HEADS // N_KV_HEADS
W_A = N_HEADS * HEAD_DIM
KV = N_KV_HEADS * HEAD_DIM
IDX_HEADS = 16
IDX_DIM = 64
TOPK_MAX = 256
Q_BLOCK = 256
PAGE_SIZE = 128
W_B = 1024
CONV_W = 3
W_C = 1024
CHUNK = 128
C_GROUPS = 8
EPS = 1e-6
NEG = -1e30
BIG = 3e38

LANE = 128
KEY_CHUNK = 256
PAGES_PER_STEP = 32
INT_MIN = -(2 ** 31)
INT_MAX = 2 ** 31 - 1
INT16_MAX = 2 ** 15 - 1
LOG2_E = 1.4426950408889634
BF16_SUBLANES = 16
VT_ROWS = HEAD_DIM + BF16_SUBLANES
COUNT_ROWS = 64
WT_CHUNK = 256
PROJ_TM = 2048

ROW_Q = 0
ROW_KV = ROW_Q + W_A
ROW_IQ = ROW_KV + 2 * KV
ROW_IK = ROW_IQ + IDX_HEADS * IDX_DIM
ROW_REST = ROW_IK + IDX_DIM + IDX_HEADS

KVI_W = KV + KV + 3 * LANE
KVI_IK = 2 * KV
KVI_IW = 2 * KV + 2 * LANE

F32 = jnp.float32
BF16 = jnp.bfloat16


def _silu(x):
    return x / (1.0 + jnp.exp(-x))


def _sigmoid(x):
    return 1.0 / (1.0 + jnp.exp(-x))


def _dot_nt(a, b):
    return lax.dot_general(a, b, (((1,), (1,)), ((), ())), preferred_element_type=F32)


def _ordered_key(x):
    bits = pltpu.bitcast(x, jnp.int32)
    return jnp.where(bits < 0, bits ^ jnp.int32(0x7FFFFFFF), bits)


def _key_to_float(key):
    bits = jnp.where(key < 0, key ^ jnp.int32(0x7FFFFFFF), key)
    return pltpu.bitcast(bits, F32)


def _kth_largest_key(count_ge, shape, k, n_total, count_hi=None, make_count_lo=None):
    if count_hi is None:
        count_coarse, n_coarse = count_ge, 0
    else:
        count_coarse, n_coarse = (lambda c: count_hi((c >> 16).astype(jnp.int16))), 15
    zero = jnp.zeros(shape, jnp.int32)
    c0 = count_coarse(zero)
    t = jnp.where(c0 >= k, zero, jnp.full(shape, INT_MIN, jnp.int32))
    cnt_t = jnp.where(c0 >= k, c0, n_total)

    def step(count):
        def body(it, carry):
            t, cnt_t = carry
            cand = t | jnp.left_shift(jnp.int32(1), jnp.int32(30) - it)
            cnt = count(cand)
            keep = cnt >= k
            return jnp.where(keep, cand, t), jnp.where(keep, cnt, cnt_t)
        return body

    carry = lax.fori_loop(0, n_coarse, step(count_coarse), (t, cnt_t))
    count_fine = count_ge if make_count_lo is None else make_count_lo(carry[0])
    return lax.fori_loop(n_coarse, 31, step(count_fine), carry)


def _rmsnorm_kernel(x_ref, g_ref, o_ref):
    x = x_ref[...]
    ms = jnp.mean(x * x, axis=-1, keepdims=True)
    o_ref[...] = (x * lax.rsqrt(ms + EPS) * g_ref[...]).astype(o_ref.dtype)


def _rmsnorm(x, g, out_dtype, tm):
    m, d = x.shape
    tm = min(tm, m)
    return pl.pallas_call(
        _rmsnorm_kernel,
        grid=(m // tm,),
        in_specs=[pl.BlockSpec((tm, d), lambda i: (i, 0)), pl.BlockSpec((1, d), lambda i: (0, 0))],
        out_specs=pl.BlockSpec((tm, d), lambda i: (i, 0)),
        out_shape=jax.ShapeDtypeStruct((m, d), out_dtype),
        compiler_params=pltpu.CompilerParams(dimension_semantics=("parallel",)),
        name="rmsnorm",
    )(x, g.reshape(1, d))


def _load_transposed(wt_ref, wb_ref, col0=0):
    n, d = wt_ref.shape
    for kb in range(d // WT_CHUNK):
        ks = slice(kb * WT_CHUNK, (kb + 1) * WT_CHUNK)
        wb_ref[ks, col0:col0 + n] = wt_ref[:, ks].T.astype(BF16)


def _proj_kernel(xn_ref, xs_ref, wt_ref, o_ref, os_ref, wb_ref):
    @pl.when(pl.program_id(1) == 0)
    def _():
        _load_transposed(wt_ref, wb_ref)
        os_ref[...] = jnp.dot(xs_ref[...].astype(BF16), wb_ref[...], preferred_element_type=F32)

    o_ref[...] = jnp.dot(xn_ref[...], wb_ref[...],
                         preferred_element_type=F32).astype(o_ref.dtype)


def _proj(xn, xs_n, wt, layer, row_of_tile, n_tiles, tn, tm):
    m, d = xn.shape
    nbs = xs_n.shape[0]
    tm = min(tm, m)
    return pl.pallas_call(
        _proj_kernel,
        grid=(n_tiles, m // tm),
        in_specs=[
            pl.BlockSpec((tm, d), lambda j, i: (i, 0)),
            pl.BlockSpec((nbs, d), lambda j, i: (0, 0)),
            pl.BlockSpec((None, pl.Element(tn), pl.Element(d)),
                         lambda j, i: (layer, pl.multiple_of(row_of_tile(j), 8), 0)),
        ],
        out_specs=[pl.BlockSpec((tm, tn), lambda j, i: (i, j)),
                   pl.BlockSpec((nbs, tn), lambda j, i: (0, j))],
        out_shape=[jax.ShapeDtypeStruct((m, n_tiles * tn), BF16),
                   jax.ShapeDtypeStruct((nbs, n_tiles * tn), F32)],
        scratch_shapes=[pltpu.VMEM((d, tn), BF16)],
        compiler_params=pltpu.CompilerParams(
            dimension_semantics=("arbitrary", "arbitrary"),
            vmem_limit_bytes=56 * 1024 * 1024),
        name="proj",
    )(xn, xs_n, wt)


def _proj_kvi_kernel(xn_ref, xs_ref, wkv_ref, wix_ref, of_ref, ob_ref, os_ref, wb_ref):
    @pl.when(pl.program_id(0) == 0)
    def _():
        _load_transposed(wkv_ref, wb_ref)
        d = wb_ref.shape[0]
        lane = lax.broadcasted_iota(jnp.int32, (WT_CHUNK, LANE), 1)
        for kb in range(d // WT_CHUNK):
            ks = slice(kb * WT_CHUNK, (kb + 1) * WT_CHUNK)
            t = wix_ref[:, ks].T
            r = pltpu.roll(t, LANE // 2, 1)
            wb_ref[ks, KVI_IK:KVI_IK + LANE] = jnp.where(lane < IDX_DIM, t, 0.0).astype(BF16)
            wb_ref[ks, KVI_IK + LANE:KVI_IW] = jnp.where(lane >= IDX_DIM, r, 0.0).astype(BF16)
            wb_ref[ks, KVI_IW:KVI_W] = jnp.where(lane < IDX_HEADS, r, 0.0).astype(BF16)
        os_ref[...] = jnp.dot(xs_ref[...].astype(BF16), wb_ref[...], preferred_element_type=F32)

    acc = jnp.dot(xn_ref[...], wb_ref[...], preferred_element_type=F32)
    of_ref[...] = acc
    ob_ref[...] = acc.astype(BF16)


def _proj_kvi(xn, xs_n, wt, layer, kv_row, ik_row, tm):
    m, d = xn.shape
    nbs = xs_n.shape[0]
    tm = min(tm, m)
    return pl.pallas_call(
        _proj_kvi_kernel,
        grid=(m // tm,),
        in_specs=[
            pl.BlockSpec((tm, d), lambda i: (i, 0)),
            pl.BlockSpec((nbs, d), lambda i: (0, 0)),
            pl.BlockSpec((None, pl.Element(2 * KV), pl.Element(d)), lambda i: (layer, kv_row, 0)),
            pl.BlockSpec((None, pl.Element(LANE), pl.Element(d)), lambda i: (layer, ik_row, 0)),
        ],
        out_specs=[pl.BlockSpec((tm, KVI_W), lambda i: (i, 0)),
                   pl.BlockSpec((tm, KVI_W), lambda i: (i, 0)),
                   pl.BlockSpec((nbs, KVI_W), lambda i: (0, 0))],
        out_shape=[jax.ShapeDtypeStruct((m, KVI_W), F32),
                   jax.ShapeDtypeStruct((m, KVI_W), BF16),
                   jax.ShapeDtypeStruct((nbs, KVI_W), F32)],
        scratch_shapes=[pltpu.VMEM((d, KVI_W), BF16)],
        compiler_params=pltpu.CompilerParams(
            dimension_semantics=("arbitrary",),
            vmem_limit_bytes=56 * 1024 * 1024),
        name="proj_kvi",
    )(xn, xs_n, wt, wt)


def _transpose_bf16(x):
    return x.astype(F32).T.astype(BF16)


def _dsa_prompt_kernel(q_ref, iq_ref, w_ref, k_ref, v_ref, ik_ref, ga_ref, o_ref,
                       st_ref, key_ref, khi_ref, klo_ref, iqt_ref, qgt_ref, wt_ref, vt_ref,
                       m0_ref, m1_ref, acc0_ref, acc1_ref, sba_ref, sbb_ref, *, topk):
    i = pl.program_id(1)
    qb = q_ref.shape[0]
    seq = k_ref.shape[0]
    kc = KEY_CHUNK
    nk = (i + 1) * qb
    nch = (nk + kc - 1) // kc
    qpos = i * qb + lax.broadcasted_iota(jnp.int32, (kc, qb), 1)
    krow = lax.broadcasted_iota(jnp.int32, (kc, qb), 0)

    @pl.when(i == 0)
    def _():
        def body(c, carry):
            off = pl.multiple_of(c * kc, kc)
            vt = _transpose_bf16(v_ref[pl.ds(off, kc), :])
            for g in range(N_KV_HEADS):
                vt_ref[g, 0:HEAD_DIM, pl.ds(off, kc)] = vt[g * HEAD_DIM:(g + 1) * HEAD_DIM]
                vt_ref[g, HEAD_DIM:VT_ROWS, pl.ds(off, kc)] = jnp.ones(
                    (VT_ROWS - HEAD_DIM, kc), BF16)
            return carry
        lax.fori_loop(0, seq // kc, body, 0)

    n_tiles = IDX_HEADS // 2
    for j in range(n_tiles):
        iqt_ref[:, j * qb:(j + 1) * qb] = _transpose_bf16(iq_ref[:, j * LANE:(j + 1) * LANE])
    for g in range(N_KV_HEADS):
        for hl in range(GROUP):
            h = g * GROUP + hl
            qgt_ref[g, :, hl * qb:(hl + 1) * qb] = _transpose_bf16(
                q_ref[:, h * HEAD_DIM:(h + 1) * HEAD_DIM])
    wt_ref[...] = (w_ref[...] * ((IDX_HEADS ** -0.5) * (IDX_DIM ** -0.5))).T

    def score_chunk(c):
        off = pl.multiple_of(c * kc, kc)
        acc = jnp.zeros((kc, qb), F32)
        for half in range(2):
            res = jnp.dot(ik_ref[pl.ds(off, kc), half * LANE:(half + 1) * LANE], iqt_ref[...],
                          preferred_element_type=F32)
            for j in range(n_tiles):
                h = 2 * j + half
                acc = acc + jnp.maximum(res[:, j * qb:(j + 1) * qb], 0.0) * wt_ref[h:h + 1, :]
        sc = jnp.where(off + krow <= qpos, acc, NEG)
        key = _ordered_key(sc)
        key_ref[pl.ds(off, kc), :] = key
        khi_ref[pl.ds(off, kc), :] = (key >> 16).astype(jnp.int16)

    def score_body(c2, carry):
        score_chunk(2 * c2)
        score_chunk(2 * c2 + 1)
        return carry

    lax.fori_loop(0, (nch + 1) // 2, score_body, 0)

    def count_ge(cand):
        def body(c, acc):
            off = pl.multiple_of(c * kc, kc)
            ge = jnp.where(key_ref[pl.ds(off, kc), :] >= cand, 1.0, 0.0)
            return acc + jnp.sum(ge.reshape(kc // COUNT_ROWS, COUNT_ROWS, qb), axis=0)
        acc = lax.fori_loop(0, nch, body, jnp.zeros((COUNT_ROWS, qb), F32))
        return jnp.sum(acc, axis=0, keepdims=True)

    def count16(ref, cand16):
        rows = 2 * COUNT_ROWS
        def body(c, acc):
            off = pl.multiple_of(c * kc, kc)
            ge = jnp.where(ref[pl.ds(off, kc), :] >= cand16, jnp.int16(1), jnp.int16(0))
            for r in range(kc // rows):
                acc = acc + ge[r * rows:(r + 1) * rows]
            return acc
        acc = lax.fori_loop(0, nch, body, jnp.zeros((rows, qb), jnp.int16))
        return jnp.sum(acc.astype(F32), axis=0, keepdims=True)

    def make_count_lo(t_coarse):
        hi = t_coarse >> 16
        hi16 = hi.astype(jnp.int16)
        above = jnp.where(hi >= INT16_MAX, 0.0,
                          count16(khi_ref, jnp.minimum(hi + 1, INT16_MAX).astype(jnp.int16)))

        def body(c, carry):
            off = pl.multiple_of(c * kc, kc)
            lo = ((key_ref[pl.ds(off, kc), :] & 0xFFFF) - 0x8000).astype(jnp.int16)
            same = khi_ref[pl.ds(off, kc), :] == hi16
            klo_ref[pl.ds(off, kc), :] = jnp.where(same, lo, jnp.int16(-0x8000))
            return carry
        lax.fori_loop(0, nch, body, 0)
        return lambda cand: above + count16(
            klo_ref, ((cand & 0xFFFF) - 0x8000).astype(jnp.int16))

    t_key, cnt_ge = _kth_largest_key(
        count_ge, (1, qb), float(topk), (nch * kc).astype(F32),
        count_hi=functools.partial(count16, khi_ref), make_count_lo=make_count_lo)

    at_top = t_key == INT_MAX
    cnt_gt = jnp.where(at_top, 0.0, count_ge(jnp.where(at_top, t_key, t_key + 1)))
    need = float(topk) - cnt_gt
    excess_ties = jnp.max(cnt_ge - cnt_gt - need) > 0.0

    def write_bounds(selected):
        def body(c, carry):
            off = pl.multiple_of(c * kc, kc)
            kidx = off + krow
            bound = jnp.where(selected(key_ref[pl.ds(off, kc), :], kidx), BIG, NEG)
            st_ref[pl.ds(off, kc), :] = jnp.where(kidx <= qpos, bound, NEG)
            return carry
        lax.fori_loop(0, nch, body, 0)

    @pl.when(jnp.logical_not(excess_ties))
    def _():
        write_bounds(lambda kk, kidx: kk >= t_key)

    @pl.when(excess_ties)
    def _():
        def tied_before(p):
            def body(c, acc):
                off = pl.multiple_of(c * kc, kc)
                hit = (key_ref[pl.ds(off, kc), :] == t_key) & (off + krow < p)
                hit = jnp.where(hit, 1.0, 0.0)
                return acc + jnp.sum(hit.reshape(kc // COUNT_ROWS, COUNT_ROWS, qb), axis=0)
            acc = lax.fori_loop(0, nch, body, jnp.zeros((COUNT_ROWS, qb), F32))
            return jnp.sum(acc, axis=0, keepdims=True)

        n_bits = max(1, (seq - 1).bit_length())

        def body(it, p):
            cand = p | jnp.left_shift(jnp.int32(1), jnp.int32(n_bits - 1) - it)
            return jnp.where(tied_before(cand) < need, cand, p)

        cut = lax.fori_loop(0, n_bits, body, jnp.zeros((1, qb), jnp.int32))
        write_bounds(lambda kk, kidx: (kk > t_key) | ((kk == t_key) & (kidx <= cut)))

    exp2_scale = (HEAD_DIM ** -0.5) * LOG2_E
    m_refs = (m0_ref, m1_ref)
    acc_refs = (acc0_ref, acc1_ref)
    for g in range(N_KV_HEADS):
        m_refs[g][...] = jnp.full(m_refs[g].shape, -jnp.inf, F32)
        acc_refs[g][...] = jnp.zeros(acc_refs[g].shape, F32)

    def logits(c, dst_ref):
        off = pl.multiple_of(jnp.minimum(c, nch - 1) * kc, kc)
        for g in range(N_KV_HEADS):
            dst_ref[g] = jnp.dot(
                k_ref[pl.ds(off, kc), g * HEAD_DIM:(g + 1) * HEAD_DIM], qgt_ref[g],
                preferred_element_type=F32)

    def softmax_pv(c, src_ref):
        off = pl.multiple_of(jnp.minimum(c, nch - 1) * kc, kc)
        bound = jnp.where(c < nch, st_ref[pl.ds(off, kc), :], NEG)
        bound = jnp.concatenate([bound] * GROUP, axis=1)
        for g in range(N_KV_HEADS):
            s = jnp.minimum(src_ref[g], bound)
            m_prev = m_refs[g][...]
            m_new = jnp.maximum(m_prev, jnp.max(s, axis=0, keepdims=True))
            alpha = jnp.exp2((m_prev - m_new) * exp2_scale)
            p = jnp.exp2((s - m_new) * exp2_scale)
            acc_refs[g][...] = alpha * acc_refs[g][...] + jnp.dot(
                vt_ref[g, :, pl.ds(off, kc)], p.astype(BF16), preferred_element_type=F32)
            m_refs[g][...] = m_new

    logits(0, sba_ref)

    def attn_body(c2, carry):
        c = 2 * c2
        logits(c + 1, sbb_ref)
        softmax_pv(c, sba_ref)
        logits(c + 2, sba_ref)
        softmax_pv(c + 1, sbb_ref)
        return carry

    lax.fori_loop(0, (nch + 1) // 2, attn_body, 0)
    for g in range(N_KV_HEADS):
        acc = acc_refs[g][...]
        o_t = acc[0:HEAD_DIM] / acc[HEAD_DIM:HEAD_DIM + 1]
        for hl in range(GROUP):
            h = g * GROUP + hl
            cols = slice(h * HEAD_DIM, (h + 1) * HEAD_DIM)
            gate = _silu(ga_ref[:, cols].astype(F32))
            o_ref[:, cols] = (o_t[:, hl * qb:(hl + 1) * qb].T * gate).astype(o_ref.dtype)


def _dsa_prompt(qiq, kvi_b, kvi_f, rest, batch, seq):
    m = qiq.shape[0]
    qb = Q_BLOCK
    nq = seq // qb
    topk = min(TOPK_MAX, seq // 4)
    assert seq % (2 * KEY_CHUNK) == 0 and topk <= KEY_CHUNK
    row = lambda b, i: b * nq + i
    return pl.pallas_call(
        functools.partial(_dsa_prompt_kernel, topk=topk),
        grid=(batch, nq),
        in_specs=[
            pl.BlockSpec((qb, W_A), lambda b, i: (row(b, i), 0)),
            pl.BlockSpec((qb, IDX_HEADS * IDX_DIM), lambda b, i: (row(b, i), 1)),
            pl.BlockSpec((qb, LANE), lambda b, i: (row(b, i), KVI_IW // LANE)),
            pl.BlockSpec((seq, KV), lambda b, i: (b, 0)),
            pl.BlockSpec((seq, KV), lambda b, i: (b, 1)),
            pl.BlockSpec((seq, 2 * LANE), lambda b, i: (b, KVI_IK // (2 * LANE))),
            pl.BlockSpec((qb, W_A), lambda b, i: (row(b, i), 0)),
        ],
        out_specs=pl.BlockSpec((qb, W_A), lambda b, i: (row(b, i), 0)),
        out_shape=jax.ShapeDtypeStruct((m, W_A), BF16),
        scratch_shapes=[
            pltpu.VMEM((seq, qb), F32),
            pltpu.VMEM((seq, qb), jnp.int32),
            pltpu.VMEM((seq, qb), jnp.int16),
            pltpu.VMEM((seq, qb), jnp.int16),
            pltpu.VMEM((LANE, IDX_HEADS // 2 * qb), BF16),
            pltpu.VMEM((N_KV_HEADS, HEAD_DIM, GROUP * qb), BF16),
            pltpu.VMEM((LANE, qb), F32),
            pltpu.VMEM((N_KV_HEADS, VT_ROWS, seq), BF16),
            pltpu.VMEM((1, GROUP * qb), F32),
            pltpu.VMEM((1, GROUP * qb), F32),
            pltpu.VMEM((VT_ROWS, GROUP * qb), F32),
            pltpu.VMEM((VT_ROWS, GROUP * qb), F32),
            pltpu.VMEM((N_KV_HEADS, KEY_CHUNK, GROUP * qb), F32),
            pltpu.VMEM((N_KV_HEADS, KEY_CHUNK, GROUP * qb), F32),
        ],
        compiler_params=pltpu.CompilerParams(
            dimension_semantics=("parallel", "arbitrary"),
            vmem_limit_bytes=56 * 1024 * 1024),
        name="dsa_prompt",
    )(qiq, qiq, kvi_f, kvi_b, kvi_b, kvi_b, rest)


PREV_ROWS = 16


def _bc_prompt_kernel(h_ref, bg_ref, cg_ref, gb_ref, u_ref, vc_ref, gc_ref, ph_ref, pcg_ref,
                      cw_ref, lg_ref, lb_ref, ws_ref, bst_ref,
                      yb_ref, yc_ref, zt_ref, zs_ref):
    tm = h_ref.shape[0]
    j = pl.program_id(1)
    z = cg_ref[...].astype(F32) * h_ref[...].astype(F32)
    pz = pcg_ref[...].astype(F32) * ph_ref[...].astype(F32)
    pz = jnp.where(j == 0, 0.0, pz)
    zs_ref[0:PREV_ROWS, :] = pz
    zs_ref[PREV_ROWS:PREV_ROWS + tm, :] = z
    z1 = zs_ref[PREV_ROWS - 1:PREV_ROWS - 1 + tm, :]
    z2 = zs_ref[PREV_ROWS - 2:PREV_ROWS - 2 + tm, :]
    conv = cw_ref[0:1, :] * z2 + cw_ref[1:2, :] * z1 + cw_ref[2:3, :] * z
    yb = bg_ref[...].astype(F32) * conv * _silu(gb_ref[...].astype(F32))
    yb_ref[...] = yb.astype(yb_ref.dtype)
    zt_ref[...] = z[tm - 8:tm]

    vc = vc_ref[...].astype(F32)
    mu = jnp.mean(vc, axis=-1, keepdims=True)
    dv = vc - mu
    var = jnp.mean(dv * dv, axis=-1, keepdims=True)
    vn = (dv * lax.rsqrt(var + EPS) * lg_ref[...] + lb_ref[...]).astype(BF16)
    rr = lax.broadcasted_iota(jnp.int32, (CHUNK, CHUNK), 0)
    cc = lax.broadcasted_iota(jnp.int32, (CHUNK, CHUNK), 1)
    gdim = W_C // C_GROUPS
    for g in range(C_GROUPS):
        wg = jnp.where(rr >= cc, ws_ref[g], 0.0).astype(BF16)
        bias = bst_ref[:, g:g + 1]
        cols = slice(g * gdim, (g + 1) * gdim)
        for c in range(tm // CHUNK):
            rows = slice(c * CHUNK, (c + 1) * CHUNK)
            mixed = jnp.dot(wg, vn[rows, cols], preferred_element_type=F32) + bias
            yc = u_ref[rows, cols].astype(F32) * mixed * _silu(gc_ref[rows, cols].astype(F32))
            yc_ref[rows, cols] = yc.astype(yc_ref.dtype)


def _bc_prompt(rest, conv_w, ln_g, ln_b, ws, bs, batch, seq, tm):
    m = rest.shape[0]
    nt = seq // tm
    row = lambda b, j: b * nt + j
    prev = lambda b, j: jnp.maximum((b * seq + j * tm) // PREV_ROWS - 1, 0)
    col = lambda c: (lambda b, j: (row(b, j), c))
    pcol = lambda c: (lambda b, j: (prev(b, j), c))
    const2 = lambda b, j: (0, 0)
    yb, yc, zt = pl.pallas_call(
        _bc_prompt_kernel,
        grid=(batch, nt),
        in_specs=[pl.BlockSpec((tm, W_B), col(c)) for c in range(1, 8)] + [
            pl.BlockSpec((PREV_ROWS, W_B), pcol(1)),
            pl.BlockSpec((PREV_ROWS, W_B), pcol(3)),
            pl.BlockSpec((CONV_W, W_B), const2),
            pl.BlockSpec((1, W_C), const2),
            pl.BlockSpec((1, W_C), const2),
            pl.BlockSpec((C_GROUPS, CHUNK, CHUNK), lambda b, j: (0, 0, 0)),
            pl.BlockSpec((CHUNK, C_GROUPS), const2),
        ],
        out_specs=[
            pl.BlockSpec((tm, W_B), lambda b, j: (row(b, j), 0)),
            pl.BlockSpec((tm, W_C), lambda b, j: (row(b, j), 0)),
            pl.BlockSpec((8, W_B), lambda b, j: (b, 0)),
        ],
        out_shape=[
            jax.ShapeDtypeStruct((m, W_B), BF16),
            jax.ShapeDtypeStruct((m, W_C), BF16),
            jax.ShapeDtypeStruct((batch * 8, W_B), F32),
        ],
        scratch_shapes=[pltpu.VMEM((tm + PREV_ROWS, W_B), F32)],
        compiler_params=pltpu.CompilerParams(
            dimension_semantics=("parallel", "arbitrary"),
            vmem_limit_bytes=56 * 1024 * 1024),
        name="bc_prompt",
    )(*([rest] * 7), rest, rest, conv_w, ln_g.reshape(1, W_C), ln_b.reshape(1, W_C), ws, bs.T)
    return yb, yc, zt


N_BRANCH = 3


def _merge_kernel(*refs):
    y_refs, ys_refs = refs[0:3], refs[3:6]
    w_refs = refs[6:9]
    r_refs, rs_refs = refs[9:12], refs[12:15]
    o_ref, os_ref, wbf_ref = refs[15:18]

    def merged(ys, rs):
        out = None
        for b in range(N_BRANCH):
            proj = jnp.dot(ys[b][...].astype(BF16), wbf_ref[b], preferred_element_type=F32)
            term = _sigmoid(rs[b][...].astype(F32)) * proj
            out = term if out is None else out + term
        return out

    @pl.when(pl.program_id(1) == 0)
    def _():
        for b in range(N_BRANCH):
            wbf_ref[b] = w_refs[b][...].astype(BF16)
        os_ref[...] = merged(ys_refs, rs_refs)

    o_ref[...] = merged(y_refs, r_refs).astype(o_ref.dtype)


def _merge(ys, ys_s, w_branch, layer, rest, rest_s, d, tm, tn):
    m = ys[0].shape[0]
    nbs = ys_s[0].shape[0]
    tm = min(tm, m)
    nb = d // tn
    r0 = 8 * W_A // tn
    y_spec = pl.BlockSpec((tm, W_A), lambda j, i: (i, 0))
    ys_spec = pl.BlockSpec((nbs, W_A), lambda j, i: (0, 0))
    w_spec = lambda b: pl.BlockSpec((None, W_A, tn), lambda j, i: (layer, b, j))
    r_spec = lambda b: pl.BlockSpec((tm, tn), lambda j, i: (i, r0 + b * nb + j))
    rs_spec = lambda b: pl.BlockSpec((nbs, tn), lambda j, i: (0, r0 + b * nb + j))
    rng = range(N_BRANCH)
    return pl.pallas_call(
        _merge_kernel,
        grid=(nb, m // tm),
        in_specs=([y_spec] * 3 + [ys_spec] * 3 + [w_spec(b) for b in rng]
                  + [r_spec(b) for b in rng] + [rs_spec(b) for b in rng]),
        out_specs=[pl.BlockSpec((tm, tn), lambda j, i: (i, j)),
                   pl.BlockSpec((nbs, tn), lambda j, i: (0, j))],
        out_shape=[jax.ShapeDtypeStruct((m, d), BF16), jax.ShapeDtypeStruct((nbs, d), F32)],
        scratch_shapes=[pltpu.VMEM((N_BRANCH, W_A, tn), BF16)],
        compiler_params=pltpu.CompilerParams(
            dimension_semantics=("arbitrary", "arbitrary"),
            vmem_limit_bytes=56 * 1024 * 1024),
        name="merge",
    )(*ys, *ys_s, *([w_branch] * 3), *([rest] * 3), *([rest_s] * 3))


def _outproj_kernel(m_ref, ms_ref, w_ref, x_ref, xs_ref, g_ref, *rest, with_next):
    if with_next:
        gn_ref, o_ref, os_ref, xn_ref, xsn_ref, wbf_ref = rest
    else:
        o_ref, os_ref, wbf_ref = rest

    def residual(m, x):
        out = jnp.dot(m.astype(BF16), wbf_ref[...], preferred_element_type=F32)
        ms = jnp.mean(out * out, axis=-1, keepdims=True)
        return x + out * lax.rsqrt(ms + EPS) * g_ref[...]

    def next_norm(y):
        ms = jnp.mean(y * y, axis=-1, keepdims=True)
        return y * lax.rsqrt(ms + EPS) * gn_ref[...]

    @pl.when(pl.program_id(0) == 0)
    def _():
        wbf_ref[...] = w_ref[...].astype(BF16)
        ys = residual(ms_ref[...], xs_ref[...])
        os_ref[...] = ys
        if with_next:
            xsn_ref[...] = next_norm(ys)

    y = residual(m_ref[...], x_ref[...])
    o_ref[...] = y
    if with_next:
        xn_ref[...] = next_norm(y).astype(xn_ref.dtype)


def _outproj(mm, mm_s, w_out, layer, x, xs, g, g_next, tm):
    m, d = x.shape
    nbs = xs.shape[0]
    tm = min(tm, m)
    with_next = g_next is not None
    row_spec = pl.BlockSpec((tm, d), lambda i: (i, 0))
    s_spec = pl.BlockSpec((nbs, d), lambda i: (0, 0))
    g_spec = pl.BlockSpec((1, d), lambda i: (0, 0))
    out_shape = [jax.ShapeDtypeStruct((m, d), F32), jax.ShapeDtypeStruct((nbs, d), F32)]
    operands = [mm, mm_s, w_out, x, xs, g.reshape(1, d)]
    if with_next:
        out_shape += [jax.ShapeDtypeStruct((m, d), BF16), jax.ShapeDtypeStruct((nbs, d), F32)]
        operands.append(g_next.reshape(1, d))
    return pl.pallas_call(
        functools.partial(_outproj_kernel, with_next=with_next),
        grid=(m // tm,),
        in_specs=[
            row_spec, s_spec,
            pl.BlockSpec((None, d, d), lambda i: (layer, 0, 0), pipeline_mode=pl.Buffered(1)),
            row_spec, s_spec, g_spec,
        ] + ([g_spec] if with_next else []),
        out_specs=[row_spec, s_spec] * (2 if with_next else 1),
        out_shape=out_shape,
        scratch_shapes=[pltpu.VMEM((d, d), BF16)],
        compiler_params=pltpu.CompilerParams(
            dimension_semantics=("arbitrary",),
            vmem_limit_bytes=56 * 1024 * 1024),
        name="outproj",
    )(*operands)


def _sample_scores_kernel(pt_ref, *refs):
    del pt_ref
    npg = PAGES_PER_STEP
    page_refs = refs[:npg]
    iq_ref, w_ref, o_ref = refs[npg:]
    iq = iq_ref[...].astype(BF16)
    w = w_ref[...] * ((IDX_HEADS ** -0.5) * (IDX_DIM ** -0.5))
    for u in range(npg):
        s = jnp.dot(iq, page_refs[u][...].astype(BF16),
                    preferred_element_type=F32)
        o_ref[u:u + 1, :] = jnp.sum(jnp.maximum(s, 0.0) * w, axis=0, keepdims=True)


def _sample_scores(layer, page_table, cache_ik_t, iq3, w3):
    nb, n_pages = page_table.shape
    npg = PAGES_PER_STEP
    assert n_pages % npg == 0
    page_spec = lambda u: pl.BlockSpec(
        (None, None, IDX_DIM, PAGE_SIZE),
        lambda s, j, pt: (layer, pt[s, j * npg + u], 0, 0))
    grid_spec = pltpu.PrefetchScalarGridSpec(
        num_scalar_prefetch=1,
        grid=(nb, n_pages // npg),
        in_specs=[page_spec(u) for u in range(npg)] + [
            pl.BlockSpec((None, IDX_HEADS, IDX_DIM), lambda s, j, pt: (s, 0, 0)),
            pl.BlockSpec((None, IDX_HEADS, 1), lambda s, j, pt: (s, 0, 0)),
        ],
        out_specs=pl.BlockSpec((None, npg, PAGE_SIZE), lambda s, j, pt: (s, j, 0)),
    )
    return pl.pallas_call(
        _sample_scores_kernel,
        grid_spec=grid_spec,
        out_shape=jax.ShapeDtypeStruct((nb, n_pages, PAGE_SIZE), F32),
        compiler_params=pltpu.CompilerParams(dimension_semantics=("parallel", "arbitrary")),
        name="sample_scores",
    )(page_table, *([cache_ik_t] * npg), iq3, w3)


def _sample_select_kernel(sc_ref, iq_ref, ikn_ref, w_ref, sel_ref, selself_ref, *, topk):
    nb = sc_ref.shape[0]
    w = w_ref[...] * ((IDX_HEADS ** -0.5) * (IDX_DIM ** -0.5))
    s_self = jnp.sum(iq_ref[...] * ikn_ref[...], axis=-1, keepdims=True)
    sc_self = jnp.sum(jnp.maximum(s_self, 0.0) * w, axis=1, keepdims=True)
    sc = sc_ref[...]
    key = _ordered_key(sc)
    key_self = _ordered_key(sc_self)

    def count_ge(cand):
        ge = jnp.where(key >= cand, 1.0, 0.0)
        cnt = jnp.sum(jnp.sum(ge, axis=2, keepdims=True), axis=1, keepdims=True)
        return cnt + jnp.where(key_self >= cand, 1.0, 0.0)

    n_past = sc.shape[1] * sc.shape[2]
    t_key, _ = _kth_largest_key(count_ge, (nb, 1, 1), float(topk), float(n_past + 1))

    at_top = t_key == INT_MAX
    cnt_gt = jnp.where(at_top, 0.0, count_ge(jnp.where(at_top, t_key, t_key + 1)))
    need = float(topk) - cnt_gt
    pos = (lax.broadcasted_iota(jnp.int32, sc.shape, 1) * sc.shape[2]
           + lax.broadcasted_iota(jnp.int32, sc.shape, 2))
    tied = key == t_key

    def tied_before(p):
        hit = jnp.where(tied & (pos < p), 1.0, 0.0)
        return jnp.sum(jnp.sum(hit, axis=2, keepdims=True), axis=1, keepdims=True)

    n_bits = n_past.bit_length()

    def body(it, p):
        cand = p | jnp.left_shift(jnp.int32(1), jnp.int32(n_bits - 1) - it)
        return jnp.where(tied_before(cand) < need, cand, p)

    cut = lax.fori_loop(0, n_bits, body, jnp.zeros((nb, 1, 1), jnp.int32))
    sel_ref[...] = jnp.where((key > t_key) | (tied & (pos <= cut)), 1.0, 0.0)
    self_sel = (key_self > t_key) | ((key_self == t_key) & (n_past <= cut))
    selself_ref[...] = jnp.broadcast_to(jnp.where(self_sel, 1.0, 0.0), selself_ref.shape)


def _sample_select(scores, iq3, ikn3, w3, topk):
    nb, n_pages, _ = scores.shape
    return pl.pallas_call(
        functools.partial(_sample_select_kernel, topk=topk),
        out_shape=[jax.ShapeDtypeStruct((nb, n_pages, PAGE_SIZE), F32),
                   jax.ShapeDtypeStruct((nb, 1, LANE), F32)],
        name="sample_select",
    )(scores, iq3, ikn3, w3)


def _sample_attn_kernel(pt_ref, *refs):
    del pt_ref
    npg = PAGES_PER_STEP
    k_refs = refs[:npg]
    v_refs = refs[npg:2 * npg]
    sel_ref, selself_ref, q_ref, kn_ref, vn_ref, o_ref, m_ref, l_ref, acc_ref = refs[2 * npg:]
    j = pl.program_id(1)
    scale = HEAD_DIM ** -0.5
    rows2 = N_KV_HEADS * PAGE_SIZE

    @pl.when(j == 0)
    def _():
        m_ref[...] = jnp.full(m_ref.shape, -jnp.inf, F32)
        l_ref[...] = jnp.zeros(l_ref.shape, F32)
        acc_ref[...] = jnp.zeros(acc_ref.shape, F32)

    q = q_ref[...].astype(BF16)
    er = lax.broadcasted_iota(jnp.int32, (PAGE_SIZE, rows2), 0)
    ec = lax.broadcasted_iota(jnp.int32, (PAGE_SIZE, rows2), 1)
    expand = jnp.where(ec // N_KV_HEADS == er, 1.0, 0.0).astype(BF16)
    sel2 = jnp.dot(sel_ref[...].astype(BF16), expand, preferred_element_type=F32)
    hh = lax.broadcasted_iota(jnp.int32, (N_HEADS, rows2), 0)
    cc = lax.broadcasted_iota(jnp.int32, (N_HEADS, rows2), 1)
    head_ok = (hh // GROUP) == (cc % N_KV_HEADS)

    def update(s, v):
        m_prev = m_ref[...]
        m_new = jnp.maximum(m_prev, jnp.max(s, axis=-1, keepdims=True))
        alpha = jnp.exp(m_prev - m_new)
        p = jnp.exp(s - m_new)
        l_ref[...] = alpha * l_ref[...] + jnp.sum(p, axis=-1, keepdims=True)
        acc_ref[...] = alpha * acc_ref[...] + v(p)
        m_ref[...] = m_new

    logits = []
    for u in range(npg):
        s = _dot_nt(q, k_refs[u][...].astype(BF16)) * scale
        logits.append(jnp.where(head_ok & (sel2[u:u + 1, :] > 0.5), s, NEG))

    def weighted_values(p):
        pb = p.astype(BF16)
        out = jnp.zeros((N_HEADS, HEAD_DIM), F32)
        for u in range(npg):
            out = out + jnp.dot(pb[:, u * rows2:(u + 1) * rows2], v_refs[u][...].astype(BF16),
                                preferred_element_type=F32)
        return out

    update(jnp.concatenate(logits, axis=1), weighted_values)

    @pl.when(j == pl.num_programs(1) - 1)
    def _():
        hrow = lax.broadcasted_iota(jnp.int32, (N_HEADS, HEAD_DIM), 0)
        kn = kn_ref[...]
        vn = vn_ref[...]
        k_self = jnp.where(hrow < GROUP, kn[0:1, :], kn[1:2, :])
        v_self = jnp.where(hrow < GROUP, vn[0:1, :], vn[1:2, :])
        s_self = jnp.sum(q_ref[...] * k_self, axis=-1, keepdims=True) * scale
        s_self = jnp.where(selself_ref[0:1, 0:1] > 0.5, s_self, NEG)
        update(s_self, lambda p: p * v_self)
        o_ref[...] = acc_ref[...] / l_ref[...]


def _sample_attn(layer, page_table, cache_k4, cache_v4, sel, selself, q3, kn3, vn3):
    nb, n_pages = page_table.shape
    npg = PAGES_PER_STEP
    rows2 = N_KV_HEADS * PAGE_SIZE
    page_spec = lambda u: pl.BlockSpec(
        (None, None, rows2, HEAD_DIM),
        lambda s, j, pt: (layer, pt[s, j * npg + u], 0, 0))
    per_seq = lambda shape: pl.BlockSpec((None,) + shape, lambda s, j, pt: (s, 0, 0))
    grid_spec = pltpu.PrefetchScalarGridSpec(
        num_scalar_prefetch=1,
        grid=(nb, n_pages // npg),
        in_specs=[page_spec(u) for u in range(npg)] * 2 + [
            pl.BlockSpec((None, npg, PAGE_SIZE), lambda s, j, pt: (s, j, 0)),
            per_seq((1, LANE)),
            per_seq((N_HEADS, HEAD_DIM)),
            per_seq((N_KV_HEADS, HEAD_DIM)),
            per_seq((N_KV_HEADS, HEAD_DIM)),
        ],
        out_specs=per_seq((N_HEADS, HEAD_DIM)),
        scratch_shapes=[
            pltpu.VMEM((N_HEADS, 1), F32),
            pltpu.VMEM((N_HEADS, 1), F32),
            pltpu.VMEM((N_HEADS, HEAD_DIM), F32),
        ],
    )
    return pl.pallas_call(
        _sample_attn_kernel,
        grid_spec=grid_spec,
        out_shape=jax.ShapeDtypeStruct((nb, N_HEADS, HEAD_DIM), F32),
        compiler_params=pltpu.CompilerParams(dimension_semantics=("parallel", "arbitrary")),
        name="sample_attn",
    )(page_table, *([cache_k4] * npg), *([cache_v4] * npg), sel, selself, q3, kn3, vn3)


def _sample_mid_kernel(att_ref, rest_ref, s0_ref, s1_ref, cw_ref, lg_ref, lb_ref, w0_ref, b0_ref,
                       ya_ref, yb_ref, yc_ref, z_ref, vn_ref):
    blk = lambda c: rest_ref[:, c * W_A:(c + 1) * W_A]
    ga, h, bg, cg, gb, u, vc, gc = [blk(c) for c in range(8)]
    ya_ref[...] = att_ref[...] * _silu(ga)
    z = cg * h
    conv = cw_ref[0:1, :] * s0_ref[...] + cw_ref[1:2, :] * s1_ref[...] + cw_ref[2:3, :] * z
    yb_ref[...] = bg * conv * _silu(gb)
    z_ref[...] = z
    mu = jnp.mean(vc, axis=-1, keepdims=True)
    dv = vc - mu
    var = jnp.mean(dv * dv, axis=-1, keepdims=True)
    vn = dv * lax.rsqrt(var + EPS) * lg_ref[...] + lb_ref[...]
    vn_ref[...] = vn
    mixed = w0_ref[...] * vn + b0_ref[...]
    yc_ref[...] = u * mixed * _silu(gc)


def _sample_mid(att, rest, s0, s1, conv_w, ln_g, ln_b, w0, b0):
    nb = att.shape[0]
    shp = jax.ShapeDtypeStruct((nb, W_A), F32)
    return pl.pallas_call(
        _sample_mid_kernel,
        out_shape=[shp] * 5,
        name="sample_mid",
    )(att, rest, s0, s1, conv_w, ln_g.reshape(1, W_C), ln_b.reshape(1, W_C), w0, b0)


def kernel(x_prompt, x_sample, cache_k, cache_v, cache_idx_k, state_conv, page_table,
           g_pre, g_post, w_in, conv_w, sgu_ln_g, sgu_ln_b, sgu_w, sgu_b, w_branch, w_out):
    depth = w_in.shape[0]
    batch, seq, d = x_prompt.shape
    nb = x_sample.shape[0]
    n_pool = cache_k.shape[1]
    n_pages = page_table.shape[1]
    topk_s = min(TOPK_MAX, (n_pages * PAGE_SIZE + 1) // 4)
    assert x_sample.shape[1] == 1

    xp = x_prompt.reshape(batch * seq, d)
    xs = x_sample.reshape(nb, d)
    cache_k4 = cache_k.reshape(depth, n_pool, PAGE_SIZE * N_KV_HEADS, HEAD_DIM)
    cache_v4 = cache_v.reshape(depth, n_pool, PAGE_SIZE * N_KV_HEADS, HEAD_DIM)
    cache_ik_t = jnp.swapaxes(cache_idx_k, 2, 3)

    wt = jnp.swapaxes(w_in, 1, 2)
    n_rest = wt.shape[1] - ROW_REST
    tn_rest = 1024
    while n_rest % tn_rest:
        tn_rest //= 2
    assert tn_rest % LANE == 0

    outs = {name: [] for name in ("kp", "vp", "ikp", "ks", "vs", "iks", "cp", "cs", "chs")}
    xn = _rmsnorm(xp, g_pre[0], BF16, 1024)
    xs_n = _rmsnorm(xs, g_pre[0], F32, 8)
    for l in range(depth):
        qiq, qiq_s = _proj(xn, xs_n, wt, l, lambda j: jnp.where(j == 0, ROW_Q, ROW_IQ),
                           2, W_A, PROJ_TM)
        kvi_f, kvi_b, kvi_s = _proj_kvi(xn, xs_n, wt, l, ROW_KV, ROW_IK, 1024)
        rest, rest_s = _proj(xn, xs_n, wt, l, lambda j: ROW_REST + j * tn_rest,
                             n_rest // tn_rest, tn_rest, PROJ_TM)

        ya = _dsa_prompt(qiq, kvi_b, kvi_f, rest, batch, seq)
        yb, yc, zt = _bc_prompt(rest, conv_w[l], sgu_ln_g[l], sgu_ln_b[l], sgu_w[l], sgu_b[l],
                                batch, seq, min(1024, seq))
        outs["kp"].append(kvi_f[:, 0:KV].reshape(batch, seq // PAGE_SIZE, PAGE_SIZE,
                                                 N_KV_HEADS, HEAD_DIM))
        outs["vp"].append(kvi_f[:, KV:2 * KV].reshape(batch, seq // PAGE_SIZE, PAGE_SIZE,
                                                      N_KV_HEADS, HEAD_DIM))
        outs["ikp"].append(kvi_f[:, KVI_IK:KVI_IK + IDX_DIM].reshape(
            batch, seq // PAGE_SIZE, PAGE_SIZE, IDX_DIM))
        outs["cp"].append(zt.reshape(batch, 8, W_B)[:, 8 - (CONV_W - 1):])

        q3 = qiq_s[:, :W_A].reshape(nb, N_HEADS, HEAD_DIM)
        iq3 = qiq_s[:, W_A:].reshape(nb, IDX_HEADS, IDX_DIM)
        kn = kvi_s[:, 0:KV]
        vn = kvi_s[:, KV:2 * KV]
        ikn = kvi_s[:, KVI_IK:KVI_IK + IDX_DIM]
        w3 = kvi_s[:, KVI_IW:KVI_IW + IDX_HEADS].reshape(nb, IDX_HEADS, 1)
        scores = _sample_scores(l, page_table, cache_ik_t, iq3, w3)
        sel, selself = _sample_select(scores, iq3, ikn.reshape(nb, 1, IDX_DIM), w3, topk_s)
        att = _sample_attn(l, page_table, cache_k4, cache_v4, sel, selself, q3,
                           kn.reshape(nb, N_KV_HEADS, HEAD_DIM),
                           vn.reshape(nb, N_KV_HEADS, HEAD_DIM))
        gdim = W_C // C_GROUPS
        w0 = jnp.repeat(sgu_w[l][:, 0, 0], gdim).reshape(1, W_C)
        b0 = jnp.repeat(sgu_b[l][:, 0], gdim).reshape(1, W_C)
        ya_s, yb_s, yc_s, z_s, vn_s = _sample_mid(
            att.reshape(nb, W_A), rest_s, state_conv[l][:, 0], state_conv[l][:, 1],
            conv_w[l], sgu_ln_g[l], sgu_ln_b[l], w0, b0)

        mm, mm_s = _merge((ya, yb, yc), (ya_s, yb_s, yc_s), w_branch, l, rest, rest_s,
                          d, 1024, min(512, d))
        if l + 1 < depth:
            xp, xs, xn, xs_n = _outproj(mm, mm_s, w_out, l, xp, xs, g_post[l], g_pre[l + 1], 512)
        else:
            xp, xs = _outproj(mm, mm_s, w_out, l, xp, xs, g_post[l], None, 512)
        outs["ks"].append(kn.reshape(nb, 1, N_KV_HEADS, HEAD_DIM))
        outs["vs"].append(vn.reshape(nb, 1, N_KV_HEADS, HEAD_DIM))
        outs["iks"].append(ikn.reshape(nb, 1, IDX_DIM))
        outs["cs"].append(jnp.stack([state_conv[l][:, 1], z_s], axis=1))
        outs["chs"].append(vn_s.reshape(nb, 1, W_C))

    st = lambda name: jnp.stack(outs[name])
    return (xp.reshape(batch, seq, d), xs.reshape(nb, 1, d),
            st("kp"), st("vp"), st("ikp"), st("ks"), st("vs"), st("iks"),
            st("cp"), st("cs"), st("chs"))
```
